```python
import jax, jax.numpy as jnp
from jax import lax
import numpy as np

D_MODEL = 1024
BATCH = 8
SEQ = 2048
DEPTH = 4
DEC_BATCH = 32
DEC_SEQ = 1
PAST_LEN = 8192
PAGE_SIZE = 128

N_A = DEPTH // 2
N_B = DEPTH - N_A
PLE_DIM = 256
EPS = 1e-6
GLA_HEADS = 4
GLA_DK = D_MODEL // 2
GLA_DV = D_MODEL
GLA_HK = GLA_DK // GLA_HEADS
GLA_HV = GLA_DV // GLA_HEADS
GATE_RANK = 16
GATE_NORM = 16.0
CHUNK = 64
GLA_IN = 2 * GLA_DK + 2 * GLA_DV + GATE_RANK
SB_HEADS = 16
SB_HD = D_MODEL // SB_HEADS
SB_WIDTH = SB_HEADS * SB_HD
Q_BLOCK = 128
SB_BIAS_INIT = -8.0

kernel_name = 'yoco_gla_stickbreaking_step'


def rms_norm(x, g):
    xf = x.astype(jnp.float32)
    y = xf * lax.rsqrt(jnp.mean(xf * xf, axis=-1, keepdims=True) + EPS)
    return (y * g.astype(jnp.float32)).astype(x.dtype)


def gla_chunked(q, k, v, gk, s0):
    B, L, H, DK = q.shape
    DV = v.shape[-1]
    c = min(CHUNK, L)
    n = -(-L // c)
    pad = n * c - L

    def prep(a):
        a = jnp.pad(a.astype(jnp.float32), ((0, 0), (0, pad), (0, 0), (0, 0)))
        return a.reshape(B, n, c, H, a.shape[-1]).transpose(0, 3, 1, 2, 4)

    qc, kc, vc, gc = prep(q), prep(k), prep(v), prep(gk)
    b = jnp.cumsum(gc, axis=3)
    b_last = b[:, :, :, -1:, :]
    q_in = qc * jnp.exp(b)
    k_in = kc * jnp.exp(-b)
    causal = jnp.tril(jnp.ones((c, c), dtype=bool))
    att = jnp.where(causal, jnp.einsum('bhnik,bhnjk->bhnij', q_in, k_in), 0.0)
    o_intra = jnp.einsum('bhnij,bhnjv->bhniv', att, vc)
    chunk_kv = jnp.einsum('bhnck,bhncv->bhnkv', kc * jnp.exp(b_last - b), vc)
    decay = jnp.exp(b_last[:, :, :, 0, :])
    if s0 is None:
        s0 = jnp.zeros((B, H, DK, DV), jnp.float32)

    def step(s, inp):
        dcy, kv = inp
        return dcy[..., None] * s + kv, s

    s_final, s_starts = lax.scan(step, s0.astype(jnp.float32),
                                 (jnp.moveaxis(decay, 2, 0), jnp.moveaxis(chunk_kv, 2, 0)))
    s_starts = jnp.moveaxis(s_starts, 0, 2)
    o_inter = jnp.einsum('bhnck,bhnkv->bhncv', q_in, s_starts)
    o = (o_intra + o_inter).transpose(0, 2, 3, 1, 4).reshape(B, n * c, H, DV)[:, :L]
    return o, s_final


def gla_mixer(hn, w_in, w_gk2, b_gk, g_out, w_out, s0):
    B, L, _ = hn.shape
    proj = hn @ w_in
    i1, i2 = GLA_DK, 2 * GLA_DK
    i3, i4 = i2 + GLA_DV, i2 + 2 * GLA_DV
    q = proj[..., :i1].reshape(B, L, GLA_HEADS, GLA_HK) * (GLA_HK ** -0.5)
    k = proj[..., i1:i2].reshape(B, L, GLA_HEADS, GLA_HK)
    v = proj[..., i2:i3].reshape(B, L, GLA_HEADS, GLA_HV)
    g = proj[..., i3:i4]
    gk = jax.nn.log_sigmoid((proj[..., i4:] @ w_gk2 + b_gk).astype(jnp.float32)) / GATE_NORM
    gk = gk.reshape(B, L, GLA_HEADS, GLA_HK)
    o, s = gla_chunked(q, k, v, gk, s0)
    o = rms_norm(o, g_out).astype(hn.dtype).reshape(B, L, GLA_DV) * jax.nn.silu(g)
    return o @ w_out, s


def stick_breaking(q, k, v, bias, q_pos, k_pos):
    B, Lq, H, D = q.shape
    qb = min(Q_BLOCK, Lq)
    nb = -(-Lq // qb)
    pad = nb * qb - Lq
    qblocks = jnp.pad(q, ((0, 0), (0, pad), (0, 0), (0, 0))).reshape(B, nb, qb, H, D).transpose(1, 0, 2, 3, 4)
    pblocks = jnp.pad(q_pos, (0, pad)).reshape(nb, qb)
    scale = D ** -0.5
    bias_f = bias.astype(jnp.float32)[None, :, None, None]

    def one_block(args):
        qblk, pos = args
        z = jnp.einsum('bqhd,bkhd->bhqk', qblk, k).astype(jnp.float32) * scale + bias_f
        mask = k_pos[None, :] < pos[:, None]
        log_keep = jnp.where(mask, jax.nn.log_sigmoid(-z), 0.0)
        later = lax.cumsum(log_keep, axis=3, reverse=True) - log_keep
        a = jnp.where(mask, jnp.exp(jax.nn.log_sigmoid(z) + later), 0.0)
        return jnp.einsum('bhqk,bkhd->bqhd', a.astype(v.dtype), v)

    out = lax.map(one_block, (qblocks, pblocks))
    return out.transpose(1, 0, 2, 3, 4).reshape(B, nb * qb, H, D)[:, :Lq]


def sb_mixer(hn, w_in, bias, w_out, k_all, v_all, q_pos, k_pos):
    B, L, _ = hn.shape
    proj = hn @ w_in
    q = proj[..., :SB_WIDTH].reshape(B, L, SB_HEADS, SB_HD)
    g = proj[..., SB_WIDTH:]
    o = stick_breaking(q, k_all, v_all, bias, q_pos, k_pos).reshape(B, L, SB_WIDTH) * jax.nn.silu(g)
    return o @ w_out


def shared_kv(h, g_kv, w_kv):
    B, L, _ = h.shape
    kv = rms_norm(h, g_kv) @ w_kv
    k = kv[..., :SB_WIDTH].reshape(B, L, SB_HEADS, SB_HD)
    v = kv[..., SB_WIDTH:].reshape(B, L, SB_HEADS, SB_HD)
    return k, v


def ple_add(h, p_i, g, w_gate, w_up):
    gate = jax.nn.sigmoid(rms_norm(h, g) @ w_gate)
    return h + gate * (p_i @ w_up)


def run_group(x, p, gla_s0, k_past, v_past, g_pre, w_ple_gate, w_ple_up, g_ple,
              gla_w_in, gla_w_gk2, gla_b_gk, gla_g_out, gla_w_out,
              g_kv, w_kv, sb_w_in, sb_bias, sb_w_out, g_final):
    L = x.shape[1]
    past = 0 if k_past is None else k_past.shape[1]
    q_pos = past + jnp.arange(L, dtype=jnp.int32)
    k_pos = jnp.arange(past + L, dtype=jnp.int32)
    h = x
    states = []
    k_new = v_new = k_all = v_all = None
    for i in range(DEPTH):
        hn = rms_norm(h, g_pre[i])
        if i < N_A:
            s0 = None if gla_s0 is None else gla_s0[i]
            out, s = gla_mixer(hn, gla_w_in[i], gla_w_gk2[i], gla_b_gk[i], gla_g_out[i], gla_w_out[i], s0)
            states.append(s.astype(x.dtype))
        else:
            j = i - N_A
            out = sb_mixer(hn, sb_w_in[j], sb_bias[j], sb_w_out[j], k_all, v_all, q_pos, k_pos)
        h = h + out
        h = ple_add(h, p[i], g_ple[i], w_ple_gate[i], w_ple_up[i])
        if i == N_A - 1:
            k_new, v_new = shared_kv(h, g_kv, w_kv)
            if k_past is None:
                k_all, v_all = k_new, v_new
            else:
                k_all = jnp.concatenate([k_past.astype(k_new.dtype), k_new], axis=1)
                v_all = jnp.concatenate([v_past.astype(v_new.dtype), v_new], axis=1)
    return rms_norm(h, g_final), jnp.stack(states), k_new, v_new


def setup_inputs(seed: int = 0) -> dict:
    key = jax.random.key(seed)
    ks = jax.random.split(key, 24)
    n_pages = PAST_LEN // PAGE_SIZE
    n_pool = (DEC_BATCH * n_pages * 5) // 4

    def nrm(k, shape, scale=1.0):
        return jax.random.normal(k, shape, jnp.float32) * scale

    perm = jax.random.permutation(ks[5], n_pool)[:DEC_BATCH * n_pages]
    page_table = perm.reshape(DEC_BATCH, n_pages).astype(jnp.int32)
    return {
        'x_prompt': nrm(ks[0], (BATCH, SEQ, D_MODEL)),
        'x_sample': nrm(ks[1], (DEC_BATCH, DEC_SEQ, D_MODEL)),
        'state_gla': nrm(ks[2], (N_A, DEC_BATCH, GLA_HEADS, GLA_HK, GLA_HV), 0.3),
        'cache_k': nrm(ks[3], (n_pool, PAGE_SIZE, SB_HEADS, SB_HD)),
        'cache_v': nrm(ks[4], (n_pool, PAGE_SIZE, SB_HEADS, SB_HD)),
        'page_table': page_table,
        'p_prompt': nrm(ks[6], (DEPTH, BATCH, SEQ, PLE_DIM)),
        'p_sample': nrm(ks[7], (DEPTH, DEC_BATCH, DEC_SEQ, PLE_DIM)),
        'g_pre': 1.0 + nrm(ks[8], (DEPTH, D_MODEL), 0.05),
        'w_ple_gate': nrm(ks[9], (DEPTH, D_MODEL, D_MODEL), D_MODEL ** -0.5),
        'w_ple_up': nrm(ks[10], (DEPTH, PLE_DIM, D_MODEL), 0.5 * PLE_DIM ** -0.5),
        'g_ple': 1.0 + nrm(ks[11], (DEPTH, D_MODEL), 0.05),
        'gla_w_in': nrm(ks[12], (N_A, D_MODEL, GLA_IN), D_MODEL ** -0.5),
        'gla_w_gk2': nrm(ks[13], (N_A, GATE_RANK, GLA_DK), GATE_RANK ** -0.5),
        'gla_b_gk': nrm(ks[14], (N_A, GLA_DK), 0.1),
        'gla_g_out': 1.0 + nrm(ks[15], (N_A, GLA_HV), 0.05),
        'gla_w_out': nrm(ks[16], (N_A, GLA_DV, D_MODEL), 0.5 * GLA_DV ** -0.5),
        'g_kv': 1.0 + nrm(ks[17], (D_MODEL,), 0.05),
        'w_kv': nrm(ks[18], (D_MODEL, 2 * SB_WIDTH), D_MODEL ** -0.5),
        'sb_w_in': nrm(ks[19], (N_B, D_MODEL, 2 * SB_WIDTH), D_MODEL ** -0.5),
        'sb_bias': SB_BIAS_INIT + nrm(ks[22], (N_B, SB_HEADS), 0.3),
        'sb_w_out': nrm(ks[20], (N_B, SB_WIDTH, D_MODEL), 0.5 * SB_WIDTH ** -0.5),
        'g_final': 1.0 + nrm(ks[21], (D_MODEL,), 0.05),
    }


def reference(x_prompt, x_sample, state_gla, cache_k, cache_v, page_table, p_prompt, p_sample,
              g_pre, w_ple_gate, w_ple_up, g_ple, gla_w_in, gla_w_gk2, gla_b_gk, gla_g_out, gla_w_out,
              g_kv, w_kv, sb_w_in, sb_bias, sb_w_out, g_final):
    weights = (g_pre, w_ple_gate, w_ple_up, g_ple, gla_w_in, gla_w_gk2, gla_b_gk, gla_g_out, gla_w_out,
               g_kv, w_kv, sb_w_in, sb_bias, sb_w_out, g_final)
    y_prompt, state_gla_prompt, k_prompt, v_prompt = run_group(x_prompt, p_prompt, None, None, None, *weights)
    db, n_pages = page_table.shape
    k_past = cache_k[page_table].reshape(db, n_pages * PAGE_SIZE, SB_HEADS, SB_HD)
    v_past = cache_v[page_table].reshape(db, n_pages * PAGE_SIZE, SB_HEADS, SB_HD)
    y_sample, state_gla_sample, k_sample, v_sample = run_group(x_sample, p_sample, state_gla, k_past, v_past, *weights)
    return (y_prompt, y_sample, state_gla_prompt, state_gla_sample, k_prompt, v_prompt, k_sample, v_sample)
```

```python
import functools

import numpy as np
import jax
import jax.numpy as jnp
from jax import lax
from jax.experimental import pallas as pl
from jax.experimental.pallas import tpu as pltpu

F32 = jnp.float32
BF16 = jnp.bfloat16

EPS = 1e-6
GATE_NORM = 16.0
GLA_CHUNK = 64
LANES = 128
VMEM_LIMIT = 48 * 1024 * 1024

TOKEN_BLOCK = 512
GLA_BLOCK = 256
SB_BLOCK = 256
DECODE_PAGES = 8


def _dot(a, b):
    return jnp.dot(a, b, preferred_element_type=F32)


def _dot_nt(a, b):
    return lax.dot_general(a, b, (((1,), (1,)), ((), ())), preferred_element_type=F32)


def _dot_tn(a, b):
    return lax.dot_general(a, b, (((0,), (0,)), ((), ())), preferred_element_type=F32)


def _split_bf16(x):
    hi = x.astype(BF16)
    lo = (x - hi.astype(F32)).astype(BF16)
    return hi, lo


def _rms(x, g):
    ms = jnp.mean(x * x, axis=-1, keepdims=True)
    return x * lax.rsqrt(ms + EPS) * g


def _softplus_neg_abs(z):
    return jnp.log1p(jnp.exp(-jnp.abs(z)))


def _log_sigmoid(z):
    return jnp.minimum(z, 0.0) - _softplus_neg_abs(z)


def _sigmoid(z):
    return 1.0 / (1.0 + jnp.exp(-z))


def _params(*sem):
    return pltpu.CompilerParams(dimension_semantics=sem, vmem_limit_bytes=VMEM_LIMIT)


def _const_spec(shape):
    nd = len(shape)
    return pl.BlockSpec(shape, lambda *_: (0,) * nd)


def _gla_in_kernel(x_ref, g_ref, w_ref, wlr_ref, wgk2_ref, bgk_ref,
                   q_ref, k_ref, v_ref, gate_ref, gk_ref, *, dk, dv, qscale):
    hn = _rms(x_ref[...], g_ref[...]).astype(BF16)
    q_ref[...] = _dot(hn, w_ref[:, 0:dk]) * qscale
    k_ref[...] = _dot(hn, w_ref[:, dk:2 * dk])
    v_ref[...] = _dot(hn, w_ref[:, 2 * dk:2 * dk + dv]).astype(BF16)
    gate_ref[...] = _dot(hn, w_ref[:, 2 * dk + dv:2 * dk + 2 * dv])
    lr = _dot(hn, wlr_ref[...]).astype(BF16)
    gk_ref[...] = _log_sigmoid(_dot(lr, wgk2_ref[...]) + bgk_ref[...]) / GATE_NORM


def _gla_in(x, g_pre, w_main, w_lr, w_gk2, b_gk, *, dk, dv, hk):
    t, d = x.shape
    tm = min(TOKEN_BLOCK, t)
    row = lambda i: (i, 0)
    return pl.pallas_call(
        functools.partial(_gla_in_kernel, dk=dk, dv=dv, qscale=float(hk) ** -0.5),
        grid=(t // tm,),
        in_specs=[pl.BlockSpec((tm, d), row), _const_spec(g_pre.shape), _const_spec(w_main.shape),
                  _const_spec(w_lr.shape), _const_spec(w_gk2.shape), _const_spec(b_gk.shape)],
        out_specs=[pl.BlockSpec((tm, dk), row), pl.BlockSpec((tm, dk), row), pl.BlockSpec((tm, dv), row),
                   pl.BlockSpec((tm, dv), row), pl.BlockSpec((tm, dk), row)],
        out_shape=[jax.ShapeDtypeStruct((t, dk), F32), jax.ShapeDtypeStruct((t, dk), F32),
                   jax.ShapeDtypeStruct((t, dv), BF16), jax.ShapeDtypeStruct((t, dv), F32),
                   jax.ShapeDtypeStruct((t, dk), F32)],
        compiler_params=_params("parallel"),
        name="gla_in",
    )(x, g_pre, w_main, w_lr, w_gk2, b_gk)


def _gla_rec_kernel(q_ref, k_ref, v_ref, gk_ref, lmat_ref, o_ref, s_ref, st_ref, *, tb, chunk):
    step = pl.program_id(2)

    @pl.when(step == 0)
    def _():
        st_ref[...] = jnp.zeros_like(st_ref)

    hi, lo = _split_bf16(gk_ref[...])
    lmat = lmat_ref[...]
    bb = _dot(lmat, hi) + _dot(lmat, lo)
    b, bl = bb[:tb], bb[tb:]
    q_in = (q_ref[...] * jnp.exp(b)).astype(BF16)
    k = k_ref[...]
    k_in = (k * jnp.exp(-b)).astype(BF16)
    k_out = (k * jnp.exp(bl - b)).astype(BF16)
    v = v_ref[...]

    r = lax.broadcasted_iota(jnp.int32, (tb, tb), 0)
    c = lax.broadcasted_iota(jnp.int32, (tb, tb), 1)
    causal = (c <= r) & ((r // chunk) == (c // chunk))
    att = jnp.where(causal, _dot_nt(q_in, k_in), 0.0).astype(BF16)
    o_intra = _dot(att, v)

    st = st_ref[...]
    for ci in range(tb // chunk):
        rows = slice(ci * chunk, (ci + 1) * chunk)
        o_ref[rows, :] = o_intra[rows] + _dot_nt(q_in[rows], st.astype(BF16))
        decay = jnp.exp(bl[ci * chunk:ci * chunk + 1])
        st = st * decay + _dot_tn(v[rows], k_out[rows])
    st_ref[...] = st

    @pl.when(step == pl.num_programs(2) - 1)
    def _():
        s_ref[0, 0] = st.T


def _gla_lmat(tb, chunk):
    r = np.arange(tb)[:, None]
    c = np.arange(tb)[None, :]
    same = (r // chunk) == (c // chunk)
    return jnp.asarray(np.concatenate([same & (c <= r), same], axis=0), dtype=BF16)


def _gla_rec(q, k, v, gk, *, batch, seq, heads, hk, hv):
    tb = min(GLA_BLOCK, seq)
    nl = seq // tb
    tok = lambda b, h, l: (b * nl + l, h)
    return pl.pallas_call(
        functools.partial(_gla_rec_kernel, tb=tb, chunk=min(GLA_CHUNK, seq)),
        grid=(batch, heads, nl),
        in_specs=[pl.BlockSpec((tb, hk), tok), pl.BlockSpec((tb, hk), tok), pl.BlockSpec((tb, hv), tok),
                  pl.BlockSpec((tb, hk), tok), _const_spec((2 * tb, tb))],
        out_specs=[pl.BlockSpec((tb, hv), tok), pl.BlockSpec((1, 1, hk, hv), lambda b, h, l: (b, h, 0, 0))],
        out_shape=[jax.ShapeDtypeStruct((batch * seq, heads * hv), F32),
                   jax.ShapeDtypeStruct((batch, heads, hk, hv), F32)],
        scratch_shapes=[pltpu.VMEM((hv, hk), F32)],
        compiler_params=_params("parallel", "parallel", "arbitrary"),
        name="gla_rec",
    )(q, k, v, gk, _gla_lmat(tb, min(GLA_CHUNK, seq)))


def _gla_step_kernel(q_ref, k_ref, v_ref, gk_ref, s_ref, o_ref, so_ref, *, heads, hk, hv):
    eye = lax.broadcasted_iota(jnp.int32, (hk, hk), 0) == lax.broadcasted_iota(jnp.int32, (hk, hk), 1)

    def column(row):
        return jnp.sum(jnp.where(eye, jnp.broadcast_to(row, (hk, hk)), 0.0), axis=1, keepdims=True)

    for h in range(heads):
        ks = slice(h * hk, (h + 1) * hk)
        vs = slice(h * hv, (h + 1) * hv)
        s_new = s_ref[0, h] * column(jnp.exp(gk_ref[0, :, ks])) + column(k_ref[0, :, ks]) * v_ref[0, :, vs]
        so_ref[0, h] = s_new
        o_ref[0, :, vs] = jnp.sum(column(q_ref[0, :, ks]) * s_new, axis=0, keepdims=True)


def _gla_step(q, k, v, gk, s0, *, heads, hk, hv):
    n = q.shape[0]
    vec = lambda a: a.reshape(n, 1, a.shape[-1])
    spec = lambda w: pl.BlockSpec((1, 1, w), lambda b: (b, 0, 0))
    sspec = pl.BlockSpec((1, heads, hk, hv), lambda b: (b, 0, 0, 0))
    o, s = pl.pallas_call(
        functools.partial(_gla_step_kernel, heads=heads, hk=hk, hv=hv),
        grid=(n,),
        in_specs=[spec(heads * hk), spec(heads * hk), spec(heads * hv), spec(heads * hk), sspec],
        out_specs=[spec(heads * hv), sspec],
        out_shape=[jax.ShapeDtypeStruct((n, 1, heads * hv), F32), jax.ShapeDtypeStruct(s0.shape, F32)],
        compiler_params=_params("parallel"),
        name="gla_step",
    )(vec(q), vec(k), vec(v.astype(F32)), vec(gk), s0)
    return o.reshape(n, heads * hv), s


def _mix_out_kernel(*refs, head_norm, emit_kv, final_norm, hv):
    refs = list(refs)
    o_ref, gate_ref, h_ref, p_ref, wout_ref, gple_ref, wgate_ref, wup_ref = refs[:8]
    pos = 8
    o = o_ref[...].astype(F32)
    if head_norm:
        gout = refs[pos][...]
        pos += 1
        segs = []
        for s in range(o.shape[1] // hv):
            seg = o[:, s * hv:(s + 1) * hv]
            segs.append(seg * lax.rsqrt(jnp.mean(seg * seg, axis=-1, keepdims=True) + EPS))
        o = jnp.concatenate(segs, axis=1) * gout
    gate = gate_ref[...]
    mixed = (o * (gate * _sigmoid(gate))).astype(BF16)
    h1 = h_ref[...] + _dot(mixed, wout_ref[...])
    ple_gate = _sigmoid(_dot(_rms(h1, gple_ref[...]).astype(BF16), wgate_ref[...]))
    h2 = h1 + ple_gate * _dot(p_ref[...].astype(BF16), wup_ref[...])
    if emit_kv:
        gkv_ref, wkv_ref = refs[pos:pos + 2]
        pos += 2
    if final_norm:
        gfin_ref = refs[pos]
        pos += 1
    y_ref = refs[pos]
    pos += 1
    y_ref[...] = _rms(h2, gfin_ref[...]) if final_norm else h2
    if emit_kv == "transposed":
        kf_ref, vf_ref, kb_ref, vb_ref = refs[pos:pos + 4]
        width = kf_ref.shape[1]
        hkv = _rms(h2, gkv_ref[...]).astype(BF16)
        kk = _dot_nt(wkv_ref[0:width, :], hkv)
        vv = _dot_nt(wkv_ref[width:2 * width, :], hkv)
        kf_ref[0] = kk
        vf_ref[0] = vv
        kb_ref[0] = kk.astype(BF16)
        vb_ref[0] = vv.astype(BF16)
    elif emit_kv == "rows":
        kf_ref, vf_ref = refs[pos:pos + 2]
        width = kf_ref.shape[1]
        hkv = _rms(h2, gkv_ref[...]).astype(BF16)
        kf_ref[...] = _dot_nt(hkv, wkv_ref[0:width, :])
        vf_ref[...] = _dot_nt(hkv, wkv_ref[width:2 * width, :])


def _mix_out(o, gate, h, p, w_out, g_ple, w_gate, w_up, *, g_out=None, kv=None, g_final=None, hv=0, seq=1):
    t, d = h.shape
    tm = min(TOKEN_BLOCK // 2, t)
    row = lambda i: (i, 0)
    emit_kv = None if kv is None else ("transposed" if seq > 1 else "rows")
    args = [o, gate, h, p, w_out, g_ple, w_gate, w_up]
    in_specs = [pl.BlockSpec((tm, o.shape[1]), row), pl.BlockSpec((tm, d), row), pl.BlockSpec((tm, d), row),
                pl.BlockSpec((tm, p.shape[1]), row), _const_spec(w_out.shape), _const_spec(g_ple.shape),
                _const_spec(w_gate.shape), _const_spec(w_up.shape)]
    if g_out is not None:
        args.append(g_out)
        in_specs.append(_const_spec(g_out.shape))
    if kv is not None:
        args.extend(kv)
        in_specs.extend([_const_spec(kv[0].shape), _const_spec(kv[1].shape)])
    if g_final is not None:
        args.append(g_final)
        in_specs.append(_const_spec(g_final.shape))
    out_specs = [pl.BlockSpec((tm, d), row)]
    out_shape = [jax.ShapeDtypeStruct((t, d), F32)]
    if emit_kv == "transposed":
        width = kv[1].shape[0] // 2
        blocks = seq // tm
        out_specs += [pl.BlockSpec((1, width, tm), lambda i: (i // blocks, 0, i % blocks))] * 4
        out_shape += ([jax.ShapeDtypeStruct((t // seq, width, seq), F32)] * 2
                      + [jax.ShapeDtypeStruct((t // seq, width, seq), BF16)] * 2)
    elif emit_kv == "rows":
        width = kv[1].shape[0] // 2
        out_specs += [pl.BlockSpec((tm, width), row)] * 2
        out_shape += [jax.ShapeDtypeStruct((t, width), F32)] * 2
    return pl.pallas_call(
        functools.partial(_mix_out_kernel, head_norm=g_out is not None, emit_kv=emit_kv,
                          final_norm=g_final is not None, hv=hv),
        grid=(t // tm,),
        in_specs=in_specs, out_specs=out_specs, out_shape=out_shape,
        compiler_params=_params("parallel"),
        name="mix_out",
    )(*args)


def _sb_in_kernel(x_ref, g_ref, w_ref, q_ref, gate_ref, *, width, scale):
    hn = _rms(x_ref[...], g_ref[...]).astype(BF16)
    q_ref[...] = (_dot(hn, w_ref[:, 0:width]) * scale).astype(BF16)
    gate_ref[...] = _dot(hn, w_ref[:, width:2 * width])


def _sb_in(x, g_pre, w_in, *, hd):
    t, d = x.shape
    width = w_in.shape[1] // 2
    tm = min(TOKEN_BLOCK, t)
    row = lambda i: (i, 0)
    return pl.pallas_call(
        functools.partial(_sb_in_kernel, width=width, scale=float(hd) ** -0.5),
        grid=(t // tm,),
        in_specs=[pl.BlockSpec((tm, d), row), _const_spec(g_pre.shape), _const_spec(w_in.shape)],
        out_specs=[pl.BlockSpec((tm, width), row), pl.BlockSpec((tm, width), row)],
        out_shape=[jax.ShapeDtypeStruct((t, width), BF16), jax.ShapeDtypeStruct((t, width), F32)],
        compiler_params=_params("parallel"),
        name="sb_in",
    )(x, g_pre, w_in)


def _later_matrix(n):
    j = np.arange(n)[:, None]
    s = np.arange(n)[None, :]
    return jnp.asarray(j > s, dtype=BF16)


def _sb_prompt_kernel(bias_ref, q_ref, kt_ref, vt_ref, u_ref, o_ref, *, tq, hd):
    pair = pl.program_id(1)
    qi = pl.program_id(2)
    q = q_ref[...]
    lane = lax.broadcasted_iota(jnp.int32, (1, LANES), 1)
    u = u_ref[...]
    r = lax.broadcasted_iota(jnp.int32, (tq, tq), 0)
    c = lax.broadcasted_iota(jnp.int32, (tq, tq), 1)
    strictly_past = c < r
    heads_per_block = LANES // hd

    outs = []
    for h in range(heads_per_block):
        head_lanes = (lane >= h * hd) & (lane < (h + 1) * hd)
        qm = jnp.where(head_lanes, q, jnp.zeros_like(q))
        bias = bias_ref[pair * heads_per_block + h]

        def tile(j, carry, diagonal):
            acc, later_sum = carry
            start = pl.multiple_of(j * tq, tq)
            kt = kt_ref[0, :, pl.ds(start, tq)]
            vt = vt_ref[0, :, pl.ds(start, tq)]
            z = _dot(qm, kt) + bias
            sp = _softplus_neg_abs(z)
            neg_part = jnp.minimum(z, 0.0)
            log_p = neg_part - sp
            log_keep = (neg_part - z) - sp
            if diagonal:
                log_keep = jnp.where(strictly_past, log_keep, 0.0)
            hi, lo = _split_bf16(log_keep)
            later = _dot(hi, u) + _dot(lo, u) + later_sum
            a = jnp.exp(log_p + later)
            if diagonal:
                a = jnp.where(strictly_past, a, 0.0)
            acc = acc + _dot_nt(a.astype(BF16), vt)
            later_sum = later_sum + jnp.sum(log_keep, axis=1, keepdims=True)
            return acc, later_sum

        carry = tile(qi, (jnp.zeros((tq, LANES), F32), jnp.zeros((tq, 1), F32)), True)
        carry = lax.fori_loop(0, qi, lambda i, cr: tile(qi - 1 - i, cr, False), carry)
        outs.append(carry[0])

    o = outs[-1]
    for h in range(heads_per_block - 2, -1, -1):
        o = jnp.where(lane < (h + 1) * hd, outs[h], o)
    o_ref[...] = o


def _sb_prompt(q, kt, vt, bias, *, batch, seq, hd):
    t, width = q.shape
    tq = min(SB_BLOCK, seq)
    nq = seq // tq
    qspec = pl.BlockSpec((tq, LANES), lambda b, p, i: (b * nq + i, p))
    kvspec = pl.BlockSpec((1, LANES, seq), lambda b, p, i: (b, p, 0))
    return pl.pallas_call(
        functools.partial(_sb_prompt_kernel, tq=tq, hd=hd),
        grid=(batch, width // LANES, nq),
        in_specs=[pl.BlockSpec(memory_space=pltpu.SMEM), qspec, kvspec, kvspec, _const_spec((tq, tq))],
        out_specs=qspec,
        out_shape=jax.ShapeDtypeStruct((t, width), F32),
        compiler_params=_params("parallel", "parallel", "arbitrary"),
        name="sb_prompt",
    )(bias, q, kt, vt, _later_matrix(tq))


def _sb_decode_kernel(pt_ref, q_ref, bias_ref, u_ref, *refs, pages, heads, hd):
    del pt_ref
    k_refs = refs[:pages]
    v_refs = refs[pages:2 * pages]
    o_ref = refs[2 * pages]
    qb_ref, acc_ref, sum_ref, z_ref, a_ref = refs[2 * pages + 1:]
    step = pl.program_id(1)
    page = u_ref.shape[0]

    @pl.when(step == 0)
    def _():
        qb_ref[...] = jnp.broadcast_to(q_ref[0], qb_ref.shape)
        acc_ref[...] = jnp.zeros_like(acc_ref)
        sum_ref[...] = jnp.zeros_like(sum_ref)

    for h in range(heads):
        qb = qb_ref[h]
        for j in range(pages):
            prod = k_refs[pages - 1 - j][0, h] * qb
            z_ref[pl.ds(j * heads + h, 1), :] = jnp.sum(prod, axis=0, keepdims=True)

    z = z_ref[...] + bias_ref[...]
    sp = _softplus_neg_abs(z)
    neg_part = jnp.minimum(z, 0.0)
    log_p = neg_part - sp
    log_keep = (neg_part - z) - sp
    hi, lo = _split_bf16(log_keep)
    u = u_ref[...]
    later_in = _dot(hi, u) + _dot(lo, u)
    row_sum = jnp.sum(log_keep, axis=1, keepdims=True)
    later_sum = sum_ref[...]
    for j in range(pages):
        rows = slice(j * heads, (j + 1) * heads)
        a_ref[rows, :] = jnp.exp(log_p[rows] + later_in[rows] + later_sum)
        later_sum = later_sum + row_sum[rows]
    sum_ref[...] = later_sum

    for h in range(heads):
        acc = acc_ref[h]
        for j in range(pages):
            a_row = a_ref[pl.ds(j * heads + h, 1), :]
            acc = acc + jnp.broadcast_to(a_row, (hd, page)) * v_refs[pages - 1 - j][0, h]
        acc_ref[h] = acc

    @pl.when(step == pl.num_programs(1) - 1)
    def _():
        o_ref[0] = jnp.sum(acc_ref[...], axis=-1, keepdims=True)


def _sb_decode(q, cache_k, cache_v, page_table, bias, *, heads, hd):
    n, width = q.shape
    page = cache_k.shape[1]
    n_pages = page_table.shape[1]
    pages = min(DECODE_PAGES, n_pages)
    steps = n_pages // pages
    ck = jnp.transpose(cache_k, (0, 2, 3, 1))
    cv = jnp.transpose(cache_v, (0, 2, 3, 1))

    def page_spec(i):
        return pl.BlockSpec((1, heads, hd, page),
                            lambda b, s, pt: (pt[b, (steps - 1 - s) * pages + i], 0, 0, 0))

    col_spec = pl.BlockSpec((1, heads, hd, 1), lambda b, s, pt: (b, 0, 0, 0))
    grid_spec = pltpu.PrefetchScalarGridSpec(
        num_scalar_prefetch=1,
        grid=(n, steps),
        in_specs=[col_spec, pl.BlockSpec((pages * heads, 1), lambda b, s, pt: (0, 0)),
                  pl.BlockSpec((page, page), lambda b, s, pt: (0, 0))]
        + [page_spec(i) for i in range(pages)] * 2,
        out_specs=col_spec,
        scratch_shapes=[pltpu.VMEM((heads, hd, page), F32), pltpu.VMEM((heads, hd, page), F32),
                        pltpu.VMEM((heads, 1), F32), pltpu.VMEM((pages * heads, page), F32),
                        pltpu.VMEM((pages * heads, page), F32)],
    )
    o = pl.pallas_call(
        functools.partial(_sb_decode_kernel, pages=pages, heads=heads, hd=hd),
        grid_spec=grid_spec,
        out_shape=jax.ShapeDtypeStruct((n, heads, hd, 1), F32),
        compiler_params=_params("parallel", "arbitrary"),
        name="sb_decode",
    )(page_table, q.reshape(n, heads, hd, 1), jnp.tile(bias, pages).reshape(pages * heads, 1),
      _later_matrix(page), *([ck] * pages), *([cv] * pages))
    return o.reshape(n, width)


def _run_group(x, p, w, *, batch, seq, gla_state, past):
    depth = p.shape[0]
    n_a = w["gla_w_in"].shape[0]
    heads, hk, hv = w["gla_dims"]
    sb_heads, hd = w["sb_dims"]
    dk, dv = heads * hk, heads * hv
    t = batch * seq
    h = x.reshape(t, x.shape[-1])
    row = lambda a: a.reshape(1, -1)
    states = []
    k_new = v_new = k_b = v_b = None
    for i in range(depth):
        p_i = p[i].reshape(t, p.shape[-1])
        ple = (row(w["g_ple"][i]), w["w_ple_gate"][i], w["w_ple_up"][i])
        g_final = row(w["g_final"]) if i == depth - 1 else None
        if i < n_a:
            q, k, v, gate, gk = _gla_in(h, row(w["g_pre"][i]), w["gla_w_main"][i], w["gla_w_lr"][i],
                                        w["gla_w_gk2"][i], row(w["gla_b_gk"][i]), dk=dk, dv=dv, hk=hk)
            if gla_state is None:
                o, s = _gla_rec(q, k, v, gk, batch=batch, seq=seq, heads=heads, hk=hk, hv=hv)
            else:
                o, s = _gla_step(q, k, v, gk, gla_state[i], heads=heads, hk=hk, hv=hv)
            states.append(s)
            kv = (row(w["g_kv"]), w["w_kv_t"]) if i == n_a - 1 else None
            res = _mix_out(o, gate, h, p_i, w["gla_w_out"][i], *ple, g_out=row(jnp.tile(w["gla_g_out"][i], heads)),
                           kv=kv, g_final=g_final, hv=hv, seq=seq)
            h = res[0]
            if kv is not None:
                k_new, v_new = res[1:3]
                if seq > 1:
                    k_b, v_b = res[3:5]
        else:
            j = i - n_a
            q, gate = _sb_in(h, row(w["g_pre"][i]), w["sb_w_in"][j], hd=hd)
            if past is None:
                o = _sb_prompt(q, k_b, v_b, w["sb_bias"][j], batch=batch, seq=seq, hd=hd)
            else:
                o = _sb_decode(q.astype(F32), *past, w["sb_bias"][j], heads=sb_heads, hd=hd)
            h = _mix_out(o, gate, h, p_i, w["sb_w_out"][j], *ple, g_final=g_final)[0]
    return h, jnp.stack(states), k_new, v_new


def kernel(x_prompt, x_sample, state_gla, cache_k, cache_v, page_table, p_prompt, p_sample, g_pre, w_ple_gate,
           w_ple_up, g_ple, gla_w_in, gla_w_gk2, gla_b_gk, gla_g_out, gla_w_out, g_kv, w_kv, sb_w_in, sb_bias,
           sb_w_out, g_final):
    batch, seq, d = x_prompt.shape
    n_dec, dec_seq, _ = x_sample.shape
    assert dec_seq == 1
    _, _, heads, hk, hv = state_gla.shape
    sb_heads, hd = cache_k.shape[2:]
    dk, dv = heads * hk, heads * hv
    rank = gla_w_gk2.shape[1]
    main = 2 * dk + 2 * dv
    bf = lambda a: a.astype(BF16)
    w = dict(
        g_pre=g_pre, g_ple=g_ple, g_kv=g_kv, g_final=g_final, gla_b_gk=gla_b_gk, gla_g_out=gla_g_out,
        sb_bias=sb_bias, gla_w_in=gla_w_in,
        w_ple_gate=bf(w_ple_gate), w_ple_up=bf(w_ple_up), gla_w_out=bf(gla_w_out), w_kv_t=bf(w_kv.T),
        sb_w_in=bf(sb_w_in), sb_w_out=bf(sb_w_out),
        gla_w_main=bf(gla_w_in[:, :, :main]),
        gla_w_lr=bf(jnp.pad(gla_w_in[:, :, main:], ((0, 0), (0, 0), (0, LANES - rank)))),
        gla_w_gk2=bf(jnp.pad(gla_w_gk2, ((0, 0), (0, LANES - rank), (0, 0)))),
        gla_dims=(heads, hk, hv), sb_dims=(sb_heads, hd),
    )
    y_p, s_p, k_p, v_p = _run_group(x_prompt, p_prompt, w, batch=batch, seq=seq, gla_state=None, past=None)
    y_s, s_s, k_s, v_s = _run_group(x_sample, p_sample, w, batch=n_dec, seq=1, gla_state=state_gla,
                                    past=(cache_k, cache_v, page_table))
    from_t = lambda a: jnp.transpose(a.reshape(batch, sb_heads, hd, seq), (0, 3, 1, 2))
    from_rows = lambda a: a.reshape(n_dec, 1, sb_heads, hd)
    return (y_p.reshape(batch, seq, d), y_s.reshape(n_dec, 1, d), s_p, s_s,
            from_t(k_p), from_t(v_p), from_rows(k_s), from_rows(v_s))
```

```python
import functools

import numpy as np
import jax
import jax.numpy as jnp
from jax import lax
from jax.experimental import pallas as pl
from jax.experimental.pallas import tpu as pltpu

F32 = jnp.float32
BF16 = jnp.bfloat16

EPS = 1e-6
GATE_NORM = 16.0
GLA_CHUNK = 64
LANES = 128
VMEM_LIMIT = 48 * 1024 * 1024

TOKEN_BLOCK = 512
GLA_BLOCK = 256
SB_BLOCK = 256
LOG2E = 1.4426950408889634
MASKED_LOG = -1e30
DECODE_PAGES = 8


def _dot(a, b):
    return jnp.dot(a, b, preferred_element_type=F32)


def _dot_nt(a, b):
    return lax.dot_general(a, b, (((1,), (1,)), ((), ())), preferred_element_type=F32)


def _dot_tn(a, b):
    return lax.dot_general(a, b, (((0,), (0,)), ((), ())), preferred_element_type=F32)


def _split_bf16(x):
    hi = x.astype(BF16)
    lo = (x - hi.astype(F32)).astype(BF16)
    return hi, lo


def _rms(x, g):
    ms = jnp.mean(x * x, axis=-1, keepdims=True)
    return x * lax.rsqrt(ms + EPS) * g


def _neg_abs(z):
    return pltpu.bitcast(pltpu.bitcast(z, jnp.uint32) | jnp.uint32(0x80000000), F32)


def _softplus_neg_abs(z):
    return jnp.log(1.0 + jnp.exp(_neg_abs(z)))


def _log_sigmoid(z):
    return jnp.minimum(z, 0.0) - _softplus_neg_abs(z)


def _sigmoid(z):
    return 1.0 / (1.0 + jnp.exp(-z))


def _params(*sem):
    return pltpu.CompilerParams(dimension_semantics=sem, vmem_limit_bytes=VMEM_LIMIT)


def _const_spec(shape):
    nd = len(shape)
    return pl.BlockSpec(shape, lambda *_: (0,) * nd)


def _gla_in_kernel(x_ref, g_ref, w_ref, wlr_ref, wgk2_ref, bgk_ref,
                   q_ref, k_ref, v_ref, gate_ref, gk_ref, *, dk, dv, qscale):
    hn = _rms(x_ref[...], g_ref[...]).astype(BF16)
    q_ref[...] = _dot(hn, w_ref[:, 0:dk]) * qscale
    k_ref[...] = _dot(hn, w_ref[:, dk:2 * dk])
    v_ref[...] = _dot(hn, w_ref[:, 2 * dk:2 * dk + dv]).astype(BF16)
    gate_ref[...] = _dot(hn, w_ref[:, 2 * dk + dv:2 * dk + 2 * dv])
    lr = _dot(hn, wlr_ref[...]).astype(BF16)
    gk_ref[...] = _log_sigmoid(_dot(lr, wgk2_ref[...]) + bgk_ref[...]) / GATE_NORM


def _gla_in(x, g_pre, w_main, w_lr, w_gk2, b_gk, *, dk, dv, hk):
    t, d = x.shape
    tm = min(TOKEN_BLOCK, t)
    row = lambda i: (i, 0)
    return pl.pallas_call(
        functools.partial(_gla_in_kernel, dk=dk, dv=dv, qscale=float(hk) ** -0.5),
        grid=(t // tm,),
        in_specs=[pl.BlockSpec((tm, d), row), _const_spec(g_pre.shape), _const_spec(w_main.shape),
                  _const_spec(w_lr.shape), _const_spec(w_gk2.shape), _const_spec(b_gk.shape)],
        out_specs=[pl.BlockSpec((tm, dk), row), pl.BlockSpec((tm, dk), row), pl.BlockSpec((tm, dv), row),
                   pl.BlockSpec((tm, dv), row), pl.BlockSpec((tm, dk), row)],
        out_shape=[jax.ShapeDtypeStruct((t, dk), F32), jax.ShapeDtypeStruct((t, dk), F32),
                   jax.ShapeDtypeStruct((t, dv), BF16), jax.ShapeDtypeStruct((t, dv), F32),
                   jax.ShapeDtypeStruct((t, dk), F32)],
        compiler_params=_params("parallel"),
        name="gla_in",
    )(x, g_pre, w_main, w_lr, w_gk2, b_gk)


def _gla_rec_kernel(q_ref, k_ref, v_ref, gk_ref, lmat_ref, o_ref, s_ref, st_ref, *, tb, chunk):
    step = pl.program_id(2)

    @pl.when(step == 0)
    def _():
        st_ref[...] = jnp.zeros_like(st_ref)

    hi, lo = _split_bf16(gk_ref[...])
    lmat = lmat_ref[...]
    bb = _dot(lmat, hi) + _dot(lmat, lo)
    b, bl = bb[:tb], bb[tb:]
    q_in = (q_ref[...] * jnp.exp(b)).astype(BF16)
    k = k_ref[...]
    k_in = (k * jnp.exp(-b)).astype(BF16)
    k_out = (k * jnp.exp(bl - b)).astype(BF16)
    v = v_ref[...]

    r = lax.broadcasted_iota(jnp.int32, (tb, tb), 0)
    c = lax.broadcasted_iota(jnp.int32, (tb, tb), 1)
    causal = (c <= r) & ((r // chunk) == (c // chunk))
    att = jnp.where(causal, _dot_nt(q_in, k_in), 0.0).astype(BF16)
    o_intra = _dot(att, v)

    st = st_ref[...]
    for ci in range(tb // chunk):
        rows = slice(ci * chunk, (ci + 1) * chunk)
        o_ref[rows, :] = o_intra[rows] + _dot_nt(q_in[rows], st.astype(BF16))
        decay = jnp.exp(bl[ci * chunk:ci * chunk + 1])
        st = st * decay + _dot_tn(v[rows], k_out[rows])
    st_ref[...] = st

    @pl.when(step == pl.num_programs(2) - 1)
    def _():
        s_ref[0, 0] = st.T


def _gla_lmat(tb, chunk):
    r = np.arange(tb)[:, None]
    c = np.arange(tb)[None, :]
    same = (r // chunk) == (c // chunk)
    return jnp.asarray(np.concatenate([same & (c <= r), same], axis=0), dtype=BF16)


def _gla_rec(q, k, v, gk, *, batch, seq, heads, hk, hv):
    tb = min(GLA_BLOCK, seq)
    nl = seq // tb
    tok = lambda b, h, l: (b * nl + l, h)
    return pl.pallas_call(
        functools.partial(_gla_rec_kernel, tb=tb, chunk=min(GLA_CHUNK, seq)),
        grid=(batch, heads, nl),
        in_specs=[pl.BlockSpec((tb, hk), tok), pl.BlockSpec((tb, hk), tok), pl.BlockSpec((tb, hv), tok),
                  pl.BlockSpec((tb, hk), tok), _const_spec((2 * tb, tb))],
        out_specs=[pl.BlockSpec((tb, hv), tok), pl.BlockSpec((1, 1, hk, hv), lambda b, h, l: (b, h, 0, 0))],
        out_shape=[jax.ShapeDtypeStruct((batch * seq, heads * hv), F32),
                   jax.ShapeDtypeStruct((batch, heads, hk, hv), F32)],
        scratch_shapes=[pltpu.VMEM((hv, hk), F32)],
        compiler_params=_params("parallel", "parallel", "arbitrary"),
        name="gla_rec",
    )(q, k, v, gk, _gla_lmat(tb, min(GLA_CHUNK, seq)))


def _gla_step_kernel(q_ref, k_ref, v_ref, gk_ref, s_ref, o_ref, so_ref, *, heads, hk, hv):
    eye = lax.broadcasted_iota(jnp.int32, (hk, hk), 0) == lax.broadcasted_iota(jnp.int32, (hk, hk), 1)

    def column(row):
        return jnp.sum(jnp.where(eye, jnp.broadcast_to(row, (hk, hk)), 0.0), axis=1, keepdims=True)

    for h in range(heads):
        ks = slice(h * hk, (h + 1) * hk)
        vs = slice(h * hv, (h + 1) * hv)
        s_new = s_ref[0, h] * column(jnp.exp(gk_ref[0, :, ks])) + column(k_ref[0, :, ks]) * v_ref[0, :, vs]
        so_ref[0, h] = s_new
        o_ref[0, :, vs] = jnp.sum(column(q_ref[0, :, ks]) * s_new, axis=0, keepdims=True)


def _gla_step(q, k, v, gk, s0, *, heads, hk, hv):
    n = q.shape[0]
    vec = lambda a: a.reshape(n, 1, a.shape[-1])
    spec = lambda w: pl.BlockSpec((1, 1, w), lambda b: (b, 0, 0))
    sspec = pl.BlockSpec((1, heads, hk, hv), lambda b: (b, 0, 0, 0))
    o, s = pl.pallas_call(
        functools.partial(_gla_step_kernel, heads=heads, hk=hk, hv=hv),
        grid=(n,),
        in_specs=[spec(heads * hk), spec(heads * hk), spec(heads * hv), spec(heads * hk), sspec],
        out_specs=[spec(heads * hv), sspec],
        out_shape=[jax.ShapeDtypeStruct((n, 1, heads * hv), F32), jax.ShapeDtypeStruct(s0.shape, F32)],
        compiler_params=_params("parallel"),
        name="gla_step",
    )(vec(q), vec(k), vec(v.astype(F32)), vec(gk), s0)
    return o.reshape(n, heads * hv), s


def _mix_out_kernel(*refs, head_norm, emit_kv, final_norm, hv):
    refs = list(refs)
    o_ref, gate_ref, h_ref, p_ref, wout_ref, gple_ref, wgate_ref, wup_ref = refs[:8]
    pos = 8
    o = o_ref[...].astype(F32)
    if head_norm:
        gout = refs[pos][...]
        pos += 1
        segs = []
        for s in range(o.shape[1] // hv):
            seg = o[:, s * hv:(s + 1) * hv]
            segs.append(seg * lax.rsqrt(jnp.mean(seg * seg, axis=-1, keepdims=True) + EPS))
        o = jnp.concatenate(segs, axis=1) * gout
    gate = gate_ref[...]
    mixed = (o * (gate * _sigmoid(gate))).astype(BF16)
    h1 = h_ref[...] + _dot(mixed, wout_ref[...])
    ple_gate = _sigmoid(_dot(_rms(h1, gple_ref[...]).astype(BF16), wgate_ref[...]))
    h2 = h1 + ple_gate * _dot(p_ref[...].astype(BF16), wup_ref[...])
    if emit_kv:
        gkv_ref, wkv_ref = refs[pos:pos + 2]
        pos += 2
    if final_norm:
        gfin_ref = refs[pos]
        pos += 1
    y_ref = refs[pos]
    pos += 1
    y_ref[...] = _rms(h2, gfin_ref[...]) if final_norm else h2
    if emit_kv == "transposed":
        kf_ref, vf_ref, kb_ref, vb_ref = refs[pos:pos + 4]
        width = kf_ref.shape[1]
        hkv = _rms(h2, gkv_ref[...]).astype(BF16)
        kk = _dot_nt(wkv_ref[0:width, :], hkv)
        vv = _dot_nt(wkv_ref[width:2 * width, :], hkv)
        kf_ref[0] = kk
        vf_ref[0] = vv
        kb_ref[0] = kk.astype(BF16)
        vb_ref[0] = vv.astype(BF16)
    elif emit_kv == "rows":
        kf_ref, vf_ref = refs[pos:pos + 2]
        width = kf_ref.shape[1]
        hkv = _rms(h2, gkv_ref[...]).astype(BF16)
        kf_ref[...] = _dot_nt(hkv, wkv_ref[0:width, :])
        vf_ref[...] = _dot_nt(hkv, wkv_ref[width:2 * width, :])


def _mix_out(o, gate, h, p, w_out, g_ple, w_gate, w_up, *, g_out=None, kv=None, g_final=None, hv=0, seq=1):
    t, d = h.shape
    tm = min(TOKEN_BLOCK // 2, t)
    row = lambda i: (i, 0)
    emit_kv = None if kv is None else ("transposed" if seq > 1 else "rows")
    args = [o, gate, h, p, w_out, g_ple, w_gate, w_up]
    in_specs = [pl.BlockSpec((tm, o.shape[1]), row), pl.BlockSpec((tm, d), row), pl.BlockSpec((tm, d), row),
                pl.BlockSpec((tm, p.shape[1]), row), _const_spec(w_out.shape), _const_spec(g_ple.shape),
                _const_spec(w_gate.shape), _const_spec(w_up.shape)]
    if g_out is not None:
        args.append(g_out)
        in_specs.append(_const_spec(g_out.shape))
    if kv is not None:
        args.extend(kv)
        in_specs.extend([_const_spec(kv[0].shape), _const_spec(kv[1].shape)])
    if g_final is not None:
        args.append(g_final)
        in_specs.append(_const_spec(g_final.shape))
    out_specs = [pl.BlockSpec((tm, d), row)]
    out_shape = [jax.ShapeDtypeStruct((t, d), F32)]
    if emit_kv == "transposed":
        width = kv[1].shape[0] // 2
        blocks = seq // tm
        out_specs += [pl.BlockSpec((1, width, tm), lambda i: (i // blocks, 0, i % blocks))] * 4
        out_shape += ([jax.ShapeDtypeStruct((t // seq, width, seq), F32)] * 2
                      + [jax.ShapeDtypeStruct((t // seq, width, seq), BF16)] * 2)
    elif emit_kv == "rows":
        width = kv[1].shape[0] // 2
        out_specs += [pl.BlockSpec((tm, width), row)] * 2
        out_shape += [jax.ShapeDtypeStruct((t, width), F32)] * 2
    return pl.pallas_call(
        functools.partial(_mix_out_kernel, head_norm=g_out is not None, emit_kv=emit_kv,
                          final_norm=g_final is not None, hv=hv),
        grid=(t // tm,),
        in_specs=in_specs, out_specs=out_specs, out_shape=out_shape,
        compiler_params=_params("parallel"),
        name="mix_out",
    )(*args)


def _sb_in_kernel(x_ref, g_ref, w_ref, q_ref, gate_ref, *, width, scale):
    hn = _rms(x_ref[...], g_ref[...]).astype(BF16)
    q_ref[...] = (_dot(hn, w_ref[:, 0:width]) * scale).astype(BF16)
    gate_ref[...] = _dot(hn, w_ref[:, width:2 * width])


def _sb_in(x, g_pre, w_in, *, hd):
    t, d = x.shape
    width = w_in.shape[1] // 2
    tm = min(TOKEN_BLOCK, t)
    row = lambda i: (i, 0)
    return pl.pallas_call(
        functools.partial(_sb_in_kernel, width=width, scale=float(hd) ** -0.5 * LOG2E),
        grid=(t // tm,),
        in_specs=[pl.BlockSpec((tm, d), row), _const_spec(g_pre.shape), _const_spec(w_in.shape)],
        out_specs=[pl.BlockSpec((tm, width), row), pl.BlockSpec((tm, width), row)],
        out_shape=[jax.ShapeDtypeStruct((t, width), BF16), jax.ShapeDtypeStruct((t, width), F32)],
        compiler_params=_params("parallel"),
        name="sb_in",
    )(x, g_pre, w_in)


def _later_matrix(n):
    j = np.arange(n)[:, None]
    s = np.arange(n)[None, :]
    return jnp.asarray(j > s, dtype=BF16)


def _sb_prompt_kernel(bias_ref, q_ref, kt_ref, vt_ref, u_ref, o_ref, logp_ref, later_ref, rsum_ref, acc_ref,
                      sum_ref, *, tq, hd):
    pair = pl.program_id(1)
    qi = pl.program_id(2)
    q = q_ref[...]
    lane = lax.broadcasted_iota(jnp.int32, (1, LANES), 1)
    heads_per_block = LANES // hd
    heads = range(heads_per_block)
    qms = [jnp.where((lane >= h * hd) & (lane < (h + 1) * hd), q, jnp.zeros_like(q)) for h in heads]
    biases = [bias_ref[pair * heads_per_block + h] * LOG2E for h in heads]

    def key_block(ref, j):
        return ref[0, :, pl.ds(pl.multiple_of(j * tq, tq), tq)]

    def scores(j):
        kt = key_block(kt_ref, j)
        return [_dot(qms[h], kt) for h in heads]

    def prepare(zs, slot, diagonal):
        if diagonal:
            strictly_past = (lax.broadcasted_iota(jnp.int32, (tq, tq), 1)
                             < lax.broadcasted_iota(jnp.int32, (tq, tq), 0))
        for h in heads:
            z = zs[h] + biases[h]
            log_p = jnp.minimum(z, 0.0) - jnp.log2(1.0 + jnp.exp2(_neg_abs(z)))
            log_keep = log_p - z
            if diagonal:
                log_keep = jnp.where(strictly_past, log_keep, 0.0)
                log_p = jnp.where(strictly_past, log_p, MASKED_LOG)
            hi = pltpu.bitcast(pltpu.bitcast(log_keep, jnp.uint32) & jnp.uint32(0xFFFF0000), F32)
            later_ref[slot, h] = _dot(jnp.concatenate([hi, log_keep - hi], axis=1), u_ref[...])
            logp_ref[slot, h] = log_p
            rsum_ref[slot, h] = jnp.sum(log_keep, axis=1, keepdims=True)

    def finish(j, slot):
        vt = key_block(vt_ref, j)
        for h in heads:
            later_sum = sum_ref[h]
            a = jnp.exp2(logp_ref[slot, h] + later_ref[slot, h] + later_sum)
            acc_ref[h] += _dot_nt(a.astype(BF16), vt)
            sum_ref[h] = later_sum + rsum_ref[slot, h]

    acc_ref[...] = jnp.zeros_like(acc_ref)
    sum_ref[...] = jnp.zeros_like(sum_ref)
    prepare(scores(qi), 0, True)

    def body(t, _):
        slot = lax.rem(t, 2)
        zs = scores(qi - 1 - t)
        finish(qi - t, slot)
        prepare(zs, 1 - slot, False)
        return 0

    lax.fori_loop(0, qi, body, 0)
    finish(0, lax.rem(qi, 2))

    o = acc_ref[heads_per_block - 1]
    for h in range(heads_per_block - 2, -1, -1):
        o = jnp.where(lane < (h + 1) * hd, acc_ref[h], o)
    o_ref[...] = o


def _sb_prompt(q, kt, vt, bias, *, batch, seq, hd):
    t, width = q.shape
    tq = min(SB_BLOCK, seq)
    nq = seq // tq
    heads_per_block = LANES // hd
    qspec = pl.BlockSpec((tq, LANES), lambda b, p, i: (b * nq + i, p))
    kvspec = pl.BlockSpec((1, LANES, seq), lambda b, p, i: (b, p, 0))
    u2 = jnp.concatenate([_later_matrix(tq)] * 2, axis=0).astype(F32)
    return pl.pallas_call(
        functools.partial(_sb_prompt_kernel, tq=tq, hd=hd),
        grid=(batch, width // LANES, nq),
        in_specs=[pl.BlockSpec(memory_space=pltpu.SMEM), qspec, kvspec, kvspec, _const_spec((2 * tq, tq))],
        out_specs=qspec,
        out_shape=jax.ShapeDtypeStruct((t, width), F32),
        scratch_shapes=[pltpu.VMEM((2, heads_per_block, tq, tq), F32), pltpu.VMEM((2, heads_per_block, tq, tq), F32),
                        pltpu.VMEM((2, heads_per_block, tq, 1), F32), pltpu.VMEM((heads_per_block, tq, LANES), F32),
                        pltpu.VMEM((heads_per_block, tq, 1), F32)],
        compiler_params=_params("parallel", "parallel", "arbitrary"),
        name="sb_prompt",
    )(bias, q, kt, vt, u2)


def _sb_decode_kernel(pt_ref, q_ref, bias_ref, u_ref, *refs, pages, heads, hd):
    del pt_ref
    k_refs = refs[:pages]
    v_refs = refs[pages:2 * pages]
    o_ref = refs[2 * pages]
    qb_ref, acc_ref, sum_ref, z_ref, a_ref = refs[2 * pages + 1:]
    step = pl.program_id(1)
    page = u_ref.shape[0]

    @pl.when(step == 0)
    def _():
        qb_ref[...] = jnp.broadcast_to(q_ref[0], qb_ref.shape)
        acc_ref[...] = jnp.zeros_like(acc_ref)
        sum_ref[...] = jnp.zeros_like(sum_ref)

    for h in range(heads):
        qb = qb_ref[h]
        for j in range(pages):
            prod = k_refs[pages - 1 - j][0, h] * qb
            z_ref[pl.ds(j * heads + h, 1), :] = jnp.sum(prod, axis=0, keepdims=True)

    z = z_ref[...] + bias_ref[...] * LOG2E
    sp = jnp.log2(1.0 + jnp.exp2(_neg_abs(z)))
    neg_part = jnp.minimum(z, 0.0)
    log_p = neg_part - sp
    log_keep = (neg_part - z) - sp
    hi, lo = _split_bf16(log_keep)
    u = u_ref[...]
    later_in = _dot(hi, u) + _dot(lo, u)
    row_sum = jnp.sum(log_keep, axis=1, keepdims=True)
    later_sum = sum_ref[...]
    for j in range(pages):
        rows = slice(j * heads, (j + 1) * heads)
        a_ref[rows, :] = jnp.exp2(log_p[rows] + later_in[rows] + later_sum)
        later_sum = later_sum + row_sum[rows]
    sum_ref[...] = later_sum

    for h in range(heads):
        acc = acc_ref[h]
        for j in range(pages):
            a_row = a_ref[pl.ds(j * heads + h, 1), :]
            acc = acc + jnp.broadcast_to(a_row, (hd, page)) * v_refs[pages - 1 - j][0, h]
        acc_ref[h] = acc

    @pl.when(step == pl.num_programs(1) - 1)
    def _():
        o_ref[0] = jnp.sum(acc_ref[...], axis=-1, keepdims=True)


def _sb_decode(q, cache_k, cache_v, page_table, bias, *, heads, hd):
    n, width = q.shape
    page = cache_k.shape[1]
    n_pages = page_table.shape[1]
    pages = min(DECODE_PAGES, n_pages)
    steps = n_pages // pages
    ck = jnp.transpose(cache_k, (0, 2, 3, 1))
    cv = jnp.transpose(cache_v, (0, 2, 3, 1))

    def page_spec(i):
        return pl.BlockSpec((1, heads, hd, page),
                            lambda b, s, pt: (pt[b, (steps - 1 - s) * pages + i], 0, 0, 0))

    col_spec = pl.BlockSpec((1, heads, hd, 1), lambda b, s, pt: (b, 0, 0, 0))
    grid_spec = pltpu.PrefetchScalarGridSpec(
        num_scalar_prefetch=1,
        grid=(n, steps),
        in_specs=[col_spec, pl.BlockSpec((pages * heads, 1), lambda b, s, pt: (0, 0)),
                  pl.BlockSpec((page, page), lambda b, s, pt: (0, 0))]
        + [page_spec(i) for i in range(pages)] * 2,
        out_specs=col_spec,
        scratch_shapes=[pltpu.VMEM((heads, hd, page), F32), pltpu.VMEM((heads, hd, page), F32),
                        pltpu.VMEM((heads, 1), F32), pltpu.VMEM((pages * heads, page), F32),
                        pltpu.VMEM((pages * heads, page), F32)],
    )
    o = pl.pallas_call(
        functools.partial(_sb_decode_kernel, pages=pages, heads=heads, hd=hd),
        grid_spec=grid_spec,
        out_shape=jax.ShapeDtypeStruct((n, heads, hd, 1), F32),
        compiler_params=_params("parallel", "arbitrary"),
        name="sb_decode",
    )(page_table, q.reshape(n, heads, hd, 1), jnp.tile(bias, pages).reshape(pages * heads, 1),
      _later_matrix(page), *([ck] * pages), *([cv] * pages))
    return o.reshape(n, width)


def _run_group(x, p, w, *, batch, seq, gla_state, past):
    depth = p.shape[0]
    n_a = w["gla_w_in"].shape[0]
    heads, hk, hv = w["gla_dims"]
    sb_heads, hd = w["sb_dims"]
    dk, dv = heads * hk, heads * hv
    t = batch * seq
    h = x.reshape(t, x.shape[-1])
    row = lambda a: a.reshape(1, -1)
    states = []
    k_new = v_new = k_b = v_b = None
    for i in range(depth):
        p_i = p[i].reshape(t, p.shape[-1])
        ple = (row(w["g_ple"][i]), w["w_ple_gate"][i], w["w_ple_up"][i])
        g_final = row(w["g_final"]) if i == depth - 1 else None
        if i < n_a:
            q, k, v, gate, gk = _gla_in(h, row(w["g_pre"][i]), w["gla_w_main"][i], w["gla_w_lr"][i],
                                        w["gla_w_gk2"][i], row(w["gla_b_gk"][i]), dk=dk, dv=dv, hk=hk)
            if gla_state is None:
                o, s = _gla_rec(q, k, v, gk, batch=batch, seq=seq, heads=heads, hk=hk, hv=hv)
            else:
                o, s = _gla_step(q, k, v, gk, gla_state[i], heads=heads, hk=hk, hv=hv)
            states.append(s)
            kv = (row(w["g_kv"]), w["w_kv_t"]) if i == n_a - 1 else None
            res = _mix_out(o, gate, h, p_i, w["gla_w_out"][i], *ple, g_out=row(jnp.tile(w["gla_g_out"][i], heads)),
                           kv=kv, g_final=g_final, hv=hv, seq=seq)
            h = res[0]
            if kv is not None:
                k_new, v_new = res[1:3]
                if seq > 1:
                    k_b, v_b = res[3:5]
        else:
            j = i - n_a
            q, gate = _sb_in(h, row(w["g_pre"][i]), w["sb_w_in"][j], hd=hd)
            if past is None:
                o = _sb_prompt(q, k_b, v_b, w["sb_bias"][j], batch=batch, seq=seq, hd=hd)
            else:
                o = _sb_decode(q.astype(F32), *past, w["sb_bias"][j], heads=sb_heads, hd=hd)
            h = _mix_out(o, gate, h, p_i, w["sb_w_out"][j], *ple, g_final=g_final)[0]
    return h, jnp.stack(states), k_new, v_new


def kernel(x_prompt, x_sample, state_gla, cache_k, cache_v, page_table, p_prompt, p_sample, g_pre, w_ple_gate,
           w_ple_up, g_ple, gla_w_in, gla_w_gk2, gla_b_gk, gla_g_out, gla_w_out, g_kv, w_kv, sb_w_in, sb_bias,
           sb_w_out, g_final):
    batch, seq, d = x_prompt.shape
    n_dec, dec_seq, _ = x_sample.shape
    assert dec_seq == 1
    _, _, heads, hk, hv = state_gla.shape
    sb_heads, hd = cache_k.shape[2:]
    dk, dv = heads * hk, heads * hv
    rank = gla_w_gk2.shape[1]
    main = 2 * dk + 2 * dv
    bf = lambda a: a.astype(BF16)
    w = dict(
        g_pre=g_pre, g_ple=g_ple, g_kv=g_kv, g_final=g_final, gla_b_gk=gla_b_gk, gla_g_out=gla_g_out,
        sb_bias=sb_bias, gla_w_in=gla_w_in,
        w_ple_gate=bf(w_ple_gate), w_ple_up=bf(w_ple_up), gla_w_out=bf(gla_w_out), w_kv_t=bf(w_kv.T),
        sb_w_in=bf(sb_w_in), sb_w_out=bf(sb_w_out),
        gla_w_main=bf(gla_w_in[:, :, :main]),
        gla_w_lr=bf(jnp.pad(gla_w_in[:, :, main:], ((0, 0), (0, 0), (0, LANES - rank)))),
        gla_w_gk2=bf(jnp.pad(gla_w_gk2, ((0, 0), (0, LANES - rank), (0, 0)))),
        gla_dims=(heads, hk, hv), sb_dims=(sb_heads, hd),
    )
    y_p, s_p, k_p, v_p = _run_group(x_prompt, p_prompt, w, batch=batch, seq=seq, gla_state=None, past=None)
    y_s, s_s, k_s, v_s = _run_group(x_sample, p_sample, w, batch=n_dec, seq=1, gla_state=state_gla,
                                    past=(cache_k, cache_v, page_table))
    from_t = lambda a: jnp.transpose(a.reshape(batch, sb_heads, hd, seq), (0, 3, 1, 2))
    from_rows = lambda a: a.reshape(n_dec, 1, sb_heads, hd)
    return (y_p.reshape(batch, seq, d), y_s.reshape(n_dec, 1, d), s_p, s_s,
            from_t(k_p), from_t(v_p), from_rows(k_s), from_rows(v_s))
```

```python
import functools

import numpy as np
import jax
import jax.numpy as jnp
from jax import lax
from jax.experimental import pallas as pl
from jax.experimental.pallas import tpu as pltpu

F32 = jnp.float32
BF16 = jnp.bfloat16

EPS = 1e-6
GATE_NORM = 16.0
GLA_CHUNK = 64
LANES = 128
VMEM_LIMIT = 48 * 1024 * 1024

TOKEN_BLOCK = 512
GLA_BLOCK = 256
SB_BLOCK = 256
SB_GROUP = 3
LOG2E = 1.4426950408889634
MASKED_LOG = -1e30
DECODE_PAGES = 8


def _dot(a, b):
    return jnp.dot(a, b, preferred_element_type=F32)


def _dot_nt(a, b):
    return lax.dot_general(a, b, (((1,), (1,)), ((), ())), preferred_element_type=F32)


def _dot_tn(a, b):
    return lax.dot_general(a, b, (((0,), (0,)), ((), ())), preferred_element_type=F32)


def _split_bf16(x):
    hi = x.astype(BF16)
    lo = (x - hi.astype(F32)).astype(BF16)
    return hi, lo


def _rms(x, g):
    ms = jnp.mean(x * x, axis=-1, keepdims=True)
    return x * lax.rsqrt(ms + EPS) * g


def _softplus_neg_abs(z):
    return jnp.log(1.0 + jnp.exp(-jnp.abs(z)))


def _log_sigmoid(z):
    return jnp.minimum(z, 0.0) - _softplus_neg_abs(z)


def _sigmoid(z):
    return 1.0 / (1.0 + jnp.exp(-z))


def _params(*sem):
    return pltpu.CompilerParams(dimension_semantics=sem, vmem_limit_bytes=VMEM_LIMIT)


def _const_spec(shape):
    nd = len(shape)
    return pl.BlockSpec(shape, lambda *_: (0,) * nd)


def _gla_in_kernel(x_ref, g_ref, w_ref, wlr_ref, wgk2_ref, bgk_ref,
                   q_ref, k_ref, v_ref, gate_ref, gk_ref, *, dk, dv, qscale):
    hn = _rms(x_ref[...], g_ref[...]).astype(BF16)
    q_ref[...] = _dot(hn, w_ref[:, 0:dk]) * qscale
    k_ref[...] = _dot(hn, w_ref[:, dk:2 * dk])
    v_ref[...] = _dot(hn, w_ref[:, 2 * dk:2 * dk + dv]).astype(BF16)
    gate_ref[...] = _dot(hn, w_ref[:, 2 * dk + dv:2 * dk + 2 * dv])
    lr = _dot(hn, wlr_ref[...]).astype(BF16)
    gk_ref[...] = _log_sigmoid(_dot(lr, wgk2_ref[...]) + bgk_ref[...]) / GATE_NORM


def _gla_in(x, g_pre, w_main, w_lr, w_gk2, b_gk, *, dk, dv, hk):
    t, d = x.shape
    tm = min(TOKEN_BLOCK, t)
    row = lambda i: (i, 0)
    return pl.pallas_call(
        functools.partial(_gla_in_kernel, dk=dk, dv=dv, qscale=float(hk) ** -0.5),
        grid=(t // tm,),
        in_specs=[pl.BlockSpec((tm, d), row), _const_spec(g_pre.shape), _const_spec(w_main.shape),
                  _const_spec(w_lr.shape), _const_spec(w_gk2.shape), _const_spec(b_gk.shape)],
        out_specs=[pl.BlockSpec((tm, dk), row), pl.BlockSpec((tm, dk), row), pl.BlockSpec((tm, dv), row),
                   pl.BlockSpec((tm, dv), row), pl.BlockSpec((tm, dk), row)],
        out_shape=[jax.ShapeDtypeStruct((t, dk), F32), jax.ShapeDtypeStruct((t, dk), F32),
                   jax.ShapeDtypeStruct((t, dv), BF16), jax.ShapeDtypeStruct((t, dv), F32),
                   jax.ShapeDtypeStruct((t, dk), F32)],
        compiler_params=_params("parallel"),
        name="gla_in",
    )(x, g_pre, w_main, w_lr, w_gk2, b_gk)


def _gla_rec_kernel(q_ref, k_ref, v_ref, gk_ref, lmat_ref, o_ref, s_ref, st_ref, *, tb, chunk):
    step = pl.program_id(1)
    heads, hv, hk = st_ref.shape
    heads_r = range(heads)

    @pl.when(step == 0)
    def _():
        st_ref[...] = jnp.zeros_like(st_ref)

    hi, lo = _split_bf16(gk_ref[...])
    lmat = lmat_ref[...]
    bb = _dot(lmat, hi) + _dot(lmat, lo)
    b, bl = bb[:tb], bb[tb:]
    q_in = (q_ref[...] * jnp.exp(b)).astype(BF16)
    k = k_ref[...]
    k_in = (k * jnp.exp(-b)).astype(BF16)
    k_out = (k * jnp.exp(bl - b)).astype(BF16)
    ks = [slice(h * hk, (h + 1) * hk) for h in heads_r]
    vs = [slice(h * hv, (h + 1) * hv) for h in heads_r]

    r = lax.broadcasted_iota(jnp.int32, (tb, tb), 0)
    c = lax.broadcasted_iota(jnp.int32, (tb, tb), 1)
    causal = (c <= r) & ((r // chunk) == (c // chunk))
    scores = [_dot_nt(q_in[:, ks[h]], k_in[:, ks[h]]) for h in heads_r]
    atts = [jnp.where(causal, s, 0.0).astype(BF16) for s in scores]
    o_intra = [_dot(atts[h], v_ref[:, vs[h]]) for h in heads_r]

    sts = [st_ref[h] for h in heads_r]
    for ci in range(tb // chunk):
        rows = slice(ci * chunk, (ci + 1) * chunk)
        o_inter = [_dot_nt(q_in[rows, ks[h]], sts[h].astype(BF16)) for h in heads_r]
        updates = [_dot_tn(v_ref[rows, vs[h]], k_out[rows, ks[h]]) for h in heads_r]
        decay = jnp.exp(bl[ci * chunk:ci * chunk + 1])
        for h in heads_r:
            o_ref[rows, vs[h]] = o_intra[h][rows] + o_inter[h]
            sts[h] = sts[h] * decay[:, ks[h]] + updates[h]
    for h in heads_r:
        st_ref[h] = sts[h]

    @pl.when(step == pl.num_programs(1) - 1)
    def _():
        for h in heads_r:
            s_ref[0, h] = sts[h].T


def _gla_lmat(tb, chunk):
    r = np.arange(tb)[:, None]
    c = np.arange(tb)[None, :]
    same = (r // chunk) == (c // chunk)
    return jnp.asarray(np.concatenate([same & (c <= r), same], axis=0), dtype=BF16)


def _gla_rec(q, k, v, gk, *, batch, seq, heads, hk, hv):
    tb = min(GLA_BLOCK, seq)
    nl = seq // tb
    tok = lambda b, l: (b * nl + l, 0)
    dk, dv = heads * hk, heads * hv
    return pl.pallas_call(
        functools.partial(_gla_rec_kernel, tb=tb, chunk=min(GLA_CHUNK, seq)),
        grid=(batch, nl),
        in_specs=[pl.BlockSpec((tb, dk), tok), pl.BlockSpec((tb, dk), tok), pl.BlockSpec((tb, dv), tok),
                  pl.BlockSpec((tb, dk), tok), _const_spec((2 * tb, tb))],
        out_specs=[pl.BlockSpec((tb, dv), tok), pl.BlockSpec((1, heads, hk, hv), lambda b, l: (b, 0, 0, 0))],
        out_shape=[jax.ShapeDtypeStruct((batch * seq, dv), F32),
                   jax.ShapeDtypeStruct((batch, heads, hk, hv), F32)],
        scratch_shapes=[pltpu.VMEM((heads, hv, hk), F32)],
        compiler_params=_params("parallel", "arbitrary"),
        name="gla_rec",
    )(q, k, v, gk, _gla_lmat(tb, min(GLA_CHUNK, seq)))


def _gla_step_kernel(q_ref, k_ref, v_ref, gk_ref, s_ref, o_ref, so_ref, *, heads, hk, hv):
    eye = lax.broadcasted_iota(jnp.int32, (hk, hk), 0) == lax.broadcasted_iota(jnp.int32, (hk, hk), 1)

    def column(row):
        return jnp.sum(jnp.where(eye, jnp.broadcast_to(row, (hk, hk)), 0.0), axis=1, keepdims=True)

    for h in range(heads):
        ks = slice(h * hk, (h + 1) * hk)
        vs = slice(h * hv, (h + 1) * hv)
        s_new = s_ref[0, h] * column(jnp.exp(gk_ref[0, :, ks])) + column(k_ref[0, :, ks]) * v_ref[0, :, vs]
        so_ref[0, h] = s_new
        o_ref[0, :, vs] = jnp.sum(column(q_ref[0, :, ks]) * s_new, axis=0, keepdims=True)


def _gla_step(q, k, v, gk, s0, *, heads, hk, hv):
    n = q.shape[0]
    vec = lambda a: a.reshape(n, 1, a.shape[-1])
    spec = lambda w: pl.BlockSpec((1, 1, w), lambda b: (b, 0, 0))
    sspec = pl.BlockSpec((1, heads, hk, hv), lambda b: (b, 0, 0, 0))
    o, s = pl.pallas_call(
        functools.partial(_gla_step_kernel, heads=heads, hk=hk, hv=hv),
        grid=(n,),
        in_specs=[spec(heads * hk), spec(heads * hk), spec(heads * hv), spec(heads * hk), sspec],
        out_specs=[spec(heads * hv), sspec],
        out_shape=[jax.ShapeDtypeStruct((n, 1, heads * hv), F32), jax.ShapeDtypeStruct(s0.shape, F32)],
        compiler_params=_params("parallel"),
        name="gla_step",
    )(vec(q), vec(k), vec(v.astype(F32)), vec(gk), s0)
    return o.reshape(n, heads * hv), s


def _mix_out_kernel(*refs, head_norm, emit_kv, final_norm, hv):
    refs = list(refs)
    o_ref, gate_ref, h_ref, p_ref, wout_ref, gple_ref, wgate_ref, wup_ref = refs[:8]
    pos = 8
    o = o_ref[...].astype(F32)
    if head_norm:
        gout = refs[pos][...]
        pos += 1
        segs = []
        for s in range(o.shape[1] // hv):
            seg = o[:, s * hv:(s + 1) * hv]
            segs.append(seg * lax.rsqrt(jnp.mean(seg * seg, axis=-1, keepdims=True) + EPS))
        o = jnp.concatenate(segs, axis=1) * gout
    gate = gate_ref[...]
    mixed = (o * (gate * _sigmoid(gate))).astype(BF16)
    h1 = h_ref[...] + _dot(mixed, wout_ref[...])
    ple_gate = _sigmoid(_dot(_rms(h1, gple_ref[...]).astype(BF16), wgate_ref[...]))
    h2 = h1 + ple_gate * _dot(p_ref[...].astype(BF16), wup_ref[...])
    if emit_kv:
        gkv_ref, wkv_ref = refs[pos:pos + 2]
        pos += 2
    if final_norm:
        gfin_ref = refs[pos]
        pos += 1
    y_ref = refs[pos]
    pos += 1
    y_ref[...] = _rms(h2, gfin_ref[...]) if final_norm else h2
    if emit_kv == "transposed":
        kf_ref, vf_ref, kb_ref, vb_ref = refs[pos:pos + 4]
        width = kf_ref.shape[1]
        hkv = _rms(h2, gkv_ref[...]).astype(BF16)
        kk = _dot_nt(wkv_ref[0:width, :], hkv)
        vv = _dot_nt(wkv_ref[width:2 * width, :], hkv)
        kf_ref[0] = kk
        vf_ref[0] = vv
        kb_ref[0] = kk.astype(BF16)
        vb_ref[0] = vv.astype(BF16)
    elif emit_kv == "rows":
        kf_ref, vf_ref = refs[pos:pos + 2]
        width = kf_ref.shape[1]
        hkv = _rms(h2, gkv_ref[...]).astype(BF16)
        kf_ref[...] = _dot_nt(hkv, wkv_ref[0:width, :])
        vf_ref[...] = _dot_nt(hkv, wkv_ref[width:2 * width, :])


def _mix_out(o, gate, h, p, w_out, g_ple, w_gate, w_up, *, g_out=None, kv=None, g_final=None, hv=0, seq=1):
    t, d = h.shape
    tm = min(TOKEN_BLOCK // 2, t)
    row = lambda i: (i, 0)
    emit_kv = None if kv is None else ("transposed" if seq > 1 else "rows")
    args = [o, gate, h, p, w_out, g_ple, w_gate, w_up]
    in_specs = [pl.BlockSpec((tm, o.shape[1]), row), pl.BlockSpec((tm, d), row), pl.BlockSpec((tm, d), row),
                pl.BlockSpec((tm, p.shape[1]), row), _const_spec(w_out.shape), _const_spec(g_ple.shape),
                _const_spec(w_gate.shape), _const_spec(w_up.shape)]
    if g_out is not None:
        args.append(g_out)
        in_specs.append(_const_spec(g_out.shape))
    if kv is not None:
        args.extend(kv)
        in_specs.extend([_const_spec(kv[0].shape), _const_spec(kv[1].shape)])
    if g_final is not None:
        args.append(g_final)
        in_specs.append(_const_spec(g_final.shape))
    out_specs = [pl.BlockSpec((tm, d), row)]
    out_shape = [jax.ShapeDtypeStruct((t, d), F32)]
    if emit_kv == "transposed":
        width = kv[1].shape[0] // 2
        blocks = seq // tm
        out_specs += [pl.BlockSpec((1, width, tm), lambda i: (i // blocks, 0, i % blocks))] * 4
        out_shape += ([jax.ShapeDtypeStruct((t // seq, width, seq), F32)] * 2
                      + [jax.ShapeDtypeStruct((t // seq, width, seq), BF16)] * 2)
    elif emit_kv == "rows":
        width = kv[1].shape[0] // 2
        out_specs += [pl.BlockSpec((tm, width), row)] * 2
        out_shape += [jax.ShapeDtypeStruct((t, width), F32)] * 2
    return pl.pallas_call(
        functools.partial(_mix_out_kernel, head_norm=g_out is not None, emit_kv=emit_kv,
                          final_norm=g_final is not None, hv=hv),
        grid=(t // tm,),
        in_specs=in_specs, out_specs=out_specs, out_shape=out_shape,
        compiler_params=_params("parallel"),
        name="mix_out",
    )(*args)


def _sb_in_kernel(x_ref, g_ref, w_ref, q_ref, gate_ref, *, width, scale):
    hn = _rms(x_ref[...], g_ref[...]).astype(BF16)
    q_ref[...] = (_dot(hn, w_ref[:, 0:width]) * scale).astype(BF16)
    gate_ref[...] = _dot(hn, w_ref[:, width:2 * width])


def _sb_in(x, g_pre, w_in, *, hd):
    t, d = x.shape
    width = w_in.shape[1] // 2
    tm = min(TOKEN_BLOCK, t)
    row = lambda i: (i, 0)
    return pl.pallas_call(
        functools.partial(_sb_in_kernel, width=width, scale=float(hd) ** -0.5 * LOG2E),
        grid=(t // tm,),
        in_specs=[pl.BlockSpec((tm, d), row), _const_spec(g_pre.shape), _const_spec(w_in.shape)],
        out_specs=[pl.BlockSpec((tm, width), row), pl.BlockSpec((tm, width), row)],
        out_shape=[jax.ShapeDtypeStruct((t, width), BF16), jax.ShapeDtypeStruct((t, width), F32)],
        compiler_params=_params("parallel"),
        name="sb_in",
    )(x, g_pre, w_in)


def _later_matrix(n):
    j = np.arange(n)[:, None]
    s = np.arange(n)[None, :]
    return jnp.asarray(j > s, dtype=BF16)


def _sb_tile_plan(nq, group):
    tiles = [(qi, qi - t, 1 if t == 0 else 0) for qi in range(nq) for t in range(qi + 1)]
    n_groups = 2 * -(-len(tiles) // (2 * group))
    tiles += [(0, 0, 2)] * ((n_groups + 2) * group - len(tiles))
    plan = np.asarray(tiles, dtype=np.int32)
    return n_groups, jnp.asarray(plan[:, 0]), jnp.asarray(plan[:, 1]), jnp.asarray(plan[:, 2])


def _sb_prompt_kernel(bias_ref, tq_ref, tk_ref, tm_ref, q_ref, kt_ref, vt_ref, u_ref, o_ref,
                      qm_ref, bm_ref, keep_ref, logp_ref, rsum_ref, logit_ref, acc_ref, sum_ref,
                      *, blk, hd, group, n_groups):
    pair = pl.program_id(1)
    heads_per_block = LANES // hd
    heads = range(heads_per_block)
    nq = q_ref.shape[0] // blk
    lane = lax.broadcasted_iota(jnp.int32, (1, LANES), 1)

    strictly_past = (lax.broadcasted_iota(jnp.int32, (blk, blk), 1) < lax.broadcasted_iota(jnp.int32, (blk, blk), 0))
    for h in heads:
        q = q_ref[...]
        qm_ref[h] = jnp.where((lane >= h * hd) & (lane < (h + 1) * hd), q, jnp.zeros_like(q))
        bias = bias_ref[pair * heads_per_block + h] * LOG2E
        bm_ref[h, 0] = jnp.full((blk, blk), bias, F32)
        bm_ref[h, 1] = jnp.where(strictly_past, bias, MASKED_LOG)
        bm_ref[h, 2] = jnp.full((blk, blk), MASKED_LOG, F32)
    acc_ref[...] = jnp.zeros_like(acc_ref)
    sum_ref[...] = jnp.zeros_like(sum_ref)

    def block(ref, j):
        return ref[0, :, pl.ds(pl.multiple_of(j * blk, blk), blk)]

    def tiles(g):
        return [g * group + u for u in range(group)]

    def stage_a_scores(g):
        zs = []
        for n in tiles(g):
            kt = block(kt_ref, tk_ref[n])
            rows = pl.ds(pl.multiple_of(tq_ref[n] * blk, blk), blk)
            zs.append([_dot(qm_ref[h, rows, :], kt) for h in heads])
        return zs

    def stage_a_logs(g, zs, slot):
        for u, n in enumerate(tiles(g)):
            for h in heads:
                z = zs[u][h] + bm_ref[h, tm_ref[n]]
                neg_part = jnp.minimum(z, 0.0)
                neg_relu = neg_part - z
                softplus = jnp.log2(1.0 + jnp.exp2(neg_part + neg_relu))
                log_keep = neg_relu - softplus
                keep_ref[slot, u, h] = log_keep.astype(BF16)
                logp_ref[slot, u, h] = neg_part - softplus
                rsum_ref[slot, u, h] = jnp.broadcast_to(jnp.sum(log_keep, axis=1, keepdims=True), (blk, LANES))

    def stage_b(slot):
        for u in range(group):
            for h in heads:
                logit_ref[slot, u, h] = _dot(keep_ref[slot, u, h], u_ref[...]) + logp_ref[slot, u, h]

    def stage_c_values(g, slot):
        outs = []
        for u, n in enumerate(tiles(g)):
            qi = tq_ref[n]
            vt = block(vt_ref, tk_ref[n])
            for h in heads:
                later_sum = sum_ref[qi, h]
                a = jnp.exp2(logit_ref[slot, u, h] + jnp.concatenate([later_sum] * (blk // LANES), axis=1))
                outs.append(_dot_nt(a.astype(BF16), vt))
                sum_ref[qi, h] = later_sum + rsum_ref[slot, u, h]
        return outs

    def stage_c_accumulate(g, outs):
        for u, n in enumerate(tiles(g)):
            for h in heads:
                acc_ref[tq_ref[n], h] += outs[u * heads_per_block + h]

    def iteration(g, parity):
        stage_b(1 - parity)
        outs = stage_c_values(g - 2, parity)
        zs = stage_a_scores(g)
        stage_a_logs(g, zs, parity)
        stage_c_accumulate(g - 2, outs)

    stage_a_logs(0, stage_a_scores(0), 0)
    zs = stage_a_scores(1)
    stage_b(0)
    stage_a_logs(1, zs, 1)

    def body(i, _):
        iteration(2 * i + 2, 0)
        iteration(2 * i + 3, 1)
        return 0

    lax.fori_loop(0, n_groups // 2, body, 0)

    for qi in range(nq):
        o = acc_ref[qi, heads_per_block - 1]
        for h in range(heads_per_block - 2, -1, -1):
            o = jnp.where(lane < (h + 1) * hd, acc_ref[qi, h], o)
        o_ref[qi * blk:(qi + 1) * blk, :] = o


def _sb_prompt(q, kt, vt, bias, *, batch, seq, hd):
    t, width = q.shape
    blk = min(SB_BLOCK, seq)
    nq = seq // blk
    group = SB_GROUP
    heads_per_block = LANES // hd
    n_groups, tile_q, tile_k, tile_mode = _sb_tile_plan(nq, group)
    smem = pl.BlockSpec(memory_space=pltpu.SMEM)
    qspec = pl.BlockSpec((seq, LANES), lambda b, p: (b, p))
    kvspec = pl.BlockSpec((1, LANES, seq), lambda b, p: (b, p, 0))
    tile = (group, heads_per_block, blk)
    return pl.pallas_call(
        functools.partial(_sb_prompt_kernel, blk=blk, hd=hd, group=group, n_groups=n_groups),
        grid=(batch, width // LANES),
        in_specs=[smem, smem, smem, smem, qspec, kvspec, kvspec, _const_spec((blk, blk))],
        out_specs=qspec,
        out_shape=jax.ShapeDtypeStruct((t, width), F32),
        scratch_shapes=[pltpu.VMEM((heads_per_block, seq, LANES), BF16),
                        pltpu.VMEM((heads_per_block, 3, blk, blk), F32),
                        pltpu.VMEM((2,) + tile + (blk,), BF16),
                        pltpu.VMEM((2,) + tile + (blk,), F32),
                        pltpu.VMEM((2,) + tile + (LANES,), F32),
                        pltpu.VMEM((2,) + tile + (blk,), F32),
                        pltpu.VMEM((nq, heads_per_block, blk, LANES), F32),
                        pltpu.VMEM((nq, heads_per_block, blk, LANES), F32)],
        compiler_params=_params("parallel", "parallel"),
        name="sb_prompt",
    )(bias, tile_q, tile_k, tile_mode, q, kt, vt, _later_matrix(blk))


def _sb_decode_kernel(pt_ref, q_ref, bias_ref, u_ref, *refs, pages, heads, hd):
    del pt_ref
    k_refs = refs[:pages]
    v_refs = refs[pages:2 * pages]
    o_ref = refs[2 * pages]
    qb_ref, acc_ref, sum_ref, z_ref, a_ref = refs[2 * pages + 1:]
    step = pl.program_id(1)
    page = u_ref.shape[0]

    @pl.when(step == 0)
    def _():
        qb_ref[...] = jnp.broadcast_to(q_ref[0], qb_ref.shape)
        acc_ref[...] = jnp.zeros_like(acc_ref)
        sum_ref[...] = jnp.zeros_like(sum_ref)

    for h in range(heads):
        qb = qb_ref[h]
        for j in range(pages):
            prod = k_refs[pages - 1 - j][0, h] * qb
            z_ref[pl.ds(j * heads + h, 1), :] = jnp.sum(prod, axis=0, keepdims=True)

    z = z_ref[...] + bias_ref[...] * LOG2E
    sp = jnp.log2(1.0 + jnp.exp2(-jnp.abs(z)))
    neg_part = jnp.minimum(z, 0.0)
    log_p = neg_part - sp
    log_keep = (neg_part - z) - sp
    hi, lo = _split_bf16(log_keep)
    u = u_ref[...]
    later_in = _dot(hi, u) + _dot(lo, u)
    row_sum = jnp.sum(log_keep, axis=1, keepdims=True)
    later_sum = sum_ref[...]
    for j in range(pages):
        rows = slice(j * heads, (j + 1) * heads)
        a_ref[rows, :] = jnp.exp2(log_p[rows] + later_in[rows] + later_sum)
        later_sum = later_sum + row_sum[rows]
    sum_ref[...] = later_sum

    for h in range(heads):
        acc = acc_ref[h]
        for j in range(pages):
            a_row = a_ref[pl.ds(j * heads + h, 1), :]
            acc = acc + jnp.broadcast_to(a_row, (hd, page)) * v_refs[pages - 1 - j][0, h]
        acc_ref[h] = acc

    @pl.when(step == pl.num_programs(1) - 1)
    def _():
        o_ref[0] = jnp.sum(acc_ref[...], axis=-1, keepdims=True)


def _sb_decode(q, cache_k, cache_v, page_table, bias, *, heads, hd):
    n, width = q.shape
    page = cache_k.shape[1]
    n_pages = page_table.shape[1]
    pages = min(DECODE_PAGES, n_pages)
    steps = n_pages // pages
    ck = jnp.transpose(cache_k, (0, 2, 3, 1))
    cv = jnp.transpose(cache_v, (0, 2, 3, 1))

    def page_spec(i):
        return pl.BlockSpec((1, heads, hd, page),
                            lambda b, s, pt: (pt[b, (steps - 1 - s) * pages + i], 0, 0, 0))

    col_spec = pl.BlockSpec((1, heads, hd, 1), lambda b, s, pt: (b, 0, 0, 0))
    grid_spec = pltpu.PrefetchScalarGridSpec(
        num_scalar_prefetch=1,
        grid=(n, steps),
        in_specs=[col_spec, pl.BlockSpec((pages * heads, 1), lambda b, s, pt: (0, 0)),
                  pl.BlockSpec((page, page), lambda b, s, pt: (0, 0))]
        + [page_spec(i) for i in range(pages)] * 2,
        out_specs=col_spec,
        scratch_shapes=[pltpu.VMEM((heads, hd, page), F32), pltpu.VMEM((heads, hd, page), F32),
                        pltpu.VMEM((heads, 1), F32), pltpu.VMEM((pages * heads, page), F32),
                        pltpu.VMEM((pages * heads, page), F32)],
    )
    o = pl.pallas_call(
        functools.partial(_sb_decode_kernel, pages=pages, heads=heads, hd=hd),
        grid_spec=grid_spec,
        out_shape=jax.ShapeDtypeStruct((n, heads, hd, 1), F32),
        compiler_params=_params("parallel", "arbitrary"),
        name="sb_decode",
    )(page_table, q.reshape(n, heads, hd, 1), jnp.tile(bias, pages).reshape(pages * heads, 1),
      _later_matrix(page), *([ck] * pages), *([cv] * pages))
    return o.reshape(n, width)


def _run_group(x, p, w, *, batch, seq, gla_state, past):
    depth = p.shape[0]
    n_a = w["gla_w_in"].shape[0]
    heads, hk, hv = w["gla_dims"]
    sb_heads, hd = w["sb_dims"]
    dk, dv = heads * hk, heads * hv
    t = batch * seq
    h = x.reshape(t, x.shape[-1])
    row = lambda a: a.reshape(1, -1)
    states = []
    k_new = v_new = k_b = v_b = None
    for i in range(depth):
        p_i = p[i].reshape(t, p.shape[-1])
        ple = (row(w["g_ple"][i]), w["w_ple_gate"][i], w["w_ple_up"][i])
        g_final = row(w["g_final"]) if i == depth - 1 else None
        if i < n_a:
            q, k, v, gate, gk = _gla_in(h, row(w["g_pre"][i]), w["gla_w_main"][i], w["gla_w_lr"][i],
                                        w["gla_w_gk2"][i], row(w["gla_b_gk"][i]), dk=dk, dv=dv, hk=hk)
            if gla_state is None:
                o, s = _gla_rec(q, k, v, gk, batch=batch, seq=seq, heads=heads, hk=hk, hv=hv)
            else:
                o, s = _gla_step(q, k, v, gk, gla_state[i], heads=heads, hk=hk, hv=hv)
            states.append(s)
            kv = (row(w["g_kv"]), w["w_kv_t"]) if i == n_a - 1 else None
            res = _mix_out(o, gate, h, p_i, w["gla_w_out"][i], *ple, g_out=row(jnp.tile(w["gla_g_out"][i], heads)),
                           kv=kv, g_final=g_final, hv=hv, seq=seq)
            h = res[0]
            if kv is not None:
                k_new, v_new = res[1:3]
                if seq > 1:
                    k_b, v_b = res[3:5]
        else:
            j = i - n_a
            q, gate = _sb_in(h, row(w["g_pre"][i]), w["sb_w_in"][j], hd=hd)
            if past is None:
                o = _sb_prompt(q, k_b, v_b, w["sb_bias"][j], batch=batch, seq=seq, hd=hd)
            else:
                o = _sb_decode(q.astype(F32), *past, w["sb_bias"][j], heads=sb_heads, hd=hd)
            h = _mix_out(o, gate, h, p_i, w["sb_w_out"][j], *ple, g_final=g_final)[0]
    return h, jnp.stack(states), k_new, v_new


def kernel(x_prompt, x_sample, state_gla, cache_k, cache_v, page_table, p_prompt, p_sample, g_pre, w_ple_gate,
           w_ple_up, g_ple, gla_w_in, gla_w_gk2, gla_b_gk, gla_g_out, gla_w_out, g_kv, w_kv, sb_w_in, sb_bias,
           sb_w_out, g_final):
    batch, seq, d = x_prompt.shape
    n_dec, dec_seq, _ = x_sample.shape
    assert dec_seq == 1
    _, _, heads, hk, hv = state_gla.shape
    sb_heads, hd = cache_k.shape[2:]
    dk, dv = heads * hk, heads * hv
    rank = gla_w_gk2.shape[1]
    main = 2 * dk + 2 * dv
    bf = lambda a: a.astype(BF16)
    w = dict(
        g_pre=g_pre, g_ple=g_ple, g_kv=g_kv, g_final=g_final, gla_b_gk=gla_b_gk, gla_g_out=gla_g_out,
        sb_bias=sb_bias, gla_w_in=gla_w_in,
        w_ple_gate=bf(w_ple_gate), w_ple_up=bf(w_ple_up), gla_w_out=bf(gla_w_out), w_kv_t=bf(w_kv.T),
        sb_w_in=bf(sb_w_in), sb_w_out=bf(sb_w_out),
        gla_w_main=bf(gla_w_in[:, :, :main]),
        gla_w_lr=bf(jnp.pad(gla_w_in[:, :, main:], ((0, 0), (0, 0), (0, LANES - rank)))),
        gla_w_gk2=bf(jnp.pad(gla_w_gk2, ((0, 0), (0, LANES - rank), (0, 0)))),
        gla_dims=(heads, hk, hv), sb_dims=(sb_heads, hd),
    )
    y_p, s_p, k_p, v_p = _run_group(x_prompt, p_prompt, w, batch=batch, seq=seq, gla_state=None, past=None)
    y_s, s_s, k_s, v_s = _run_group(x_sample, p_sample, w, batch=n_dec, seq=1, gla_state=state_gla,
                                    past=(cache_k, cache_v, page_table))
    from_t = lambda a: jnp.transpose(a.reshape(batch, sb_heads, hd, seq), (0, 3, 1, 2))
    from_rows = lambda a: a.reshape(n_dec, 1, sb_heads, hd)
    return (y_p.reshape(batch, seq, d), y_s.reshape(n_dec, 1, d), s_p, s_s,
            from_t(k_p), from_t(v_p), from_rows(k_s), from_rows(v_s))
```

```python
import functools

import numpy as np
import jax
import jax.numpy as jnp
from jax import lax
from jax.experimental import pallas as pl
from jax.experimental.pallas import tpu as pltpu

F32 = jnp.float32
BF16 = jnp.bfloat16

EPS = 1e-6
GATE_NORM = 16.0
GLA_CHUNK = 64
LANES = 128
VMEM_LIMIT = 48 * 1024 * 1024

TOKEN_BLOCK = 512
GLA_BLOCK = 256
SB_BLOCK = 256
SB_GROUP = 3
LOG2E = 1.4426950408889634
MASKED_LOG = -1e30
DECODE_PAGES = 16


def _dot(a, b):
    return jnp.dot(a, b, preferred_element_type=F32)


def _dot_nt(a, b):
    return lax.dot_general(a, b, (((1,), (1,)), ((), ())), preferred_element_type=F32)


def _dot_tn(a, b):
    return lax.dot_general(a, b, (((0,), (0,)), ((), ())), preferred_element_type=F32)


def _split_bf16(x):
    hi = x.astype(BF16)
    lo = (x - hi.astype(F32)).astype(BF16)
    return hi, lo


def _rms(x, g):
    ms = jnp.mean(x * x, axis=-1, keepdims=True)
    return x * lax.rsqrt(ms + EPS) * g


def _softplus_neg_abs(z):
    return jnp.log(1.0 + jnp.exp(-jnp.abs(z)))


def _log_sigmoid(z):
    return jnp.minimum(z, 0.0) - _softplus_neg_abs(z)


def _sigmoid(z):
    return 1.0 / (1.0 + jnp.exp(-z))


def _params(*sem):
    return pltpu.CompilerParams(dimension_semantics=sem, vmem_limit_bytes=VMEM_LIMIT)


def _const_spec(shape):
    nd = len(shape)
    return pl.BlockSpec(shape, lambda *_: (0,) * nd)


def _gla_in_kernel(x_ref, g_ref, w_ref, wlr_ref, wgk2_ref, bgk_ref,
                   q_ref, k_ref, v_ref, gate_ref, gk_ref, *, dk, dv, qscale):
    hn = _rms(x_ref[...], g_ref[...]).astype(BF16)
    q_ref[...] = _dot(hn, w_ref[:, 0:dk]) * qscale
    k_ref[...] = _dot(hn, w_ref[:, dk:2 * dk])
    v_ref[...] = _dot(hn, w_ref[:, 2 * dk:2 * dk + dv]).astype(BF16)
    gate_ref[...] = _dot(hn, w_ref[:, 2 * dk + dv:2 * dk + 2 * dv])
    lr = _dot(hn, wlr_ref[...]).astype(BF16)
    gk_ref[...] = _log_sigmoid(_dot(lr, wgk2_ref[...]) + bgk_ref[...]) / GATE_NORM


def _gla_in(x, g_pre, w_main, w_lr, w_gk2, b_gk, *, dk, dv, hk):
    t, d = x.shape
    tm = min(TOKEN_BLOCK, t)
    row = lambda i: (i, 0)
    return pl.pallas_call(
        functools.partial(_gla_in_kernel, dk=dk, dv=dv, qscale=float(hk) ** -0.5),
        grid=(t // tm,),
        in_specs=[pl.BlockSpec((tm, d), row), _const_spec(g_pre.shape), _const_spec(w_main.shape),
                  _const_spec(w_lr.shape), _const_spec(w_gk2.shape), _const_spec(b_gk.shape)],
        out_specs=[pl.BlockSpec((tm, dk), row), pl.BlockSpec((tm, dk), row), pl.BlockSpec((tm, dv), row),
                   pl.BlockSpec((tm, dv), row), pl.BlockSpec((tm, dk), row)],
        out_shape=[jax.ShapeDtypeStruct((t, dk), F32), jax.ShapeDtypeStruct((t, dk), F32),
                   jax.ShapeDtypeStruct((t, dv), BF16), jax.ShapeDtypeStruct((t, dv), F32),
                   jax.ShapeDtypeStruct((t, dk), F32)],
        compiler_params=_params("parallel"),
        name="gla_in",
    )(x, g_pre, w_main, w_lr, w_gk2, b_gk)


def _gla_rec_kernel(q_ref, k_ref, v_ref, gk_ref, lmat_ref, o_ref, s_ref, st_ref, *, tb, chunk):
    step = pl.program_id(1)
    heads, hv, hk = st_ref.shape
    heads_r = range(heads)

    @pl.when(step == 0)
    def _():
        st_ref[...] = jnp.zeros_like(st_ref)

    hi, lo = _split_bf16(gk_ref[...])
    lmat = lmat_ref[...]
    bb = _dot(lmat, hi) + _dot(lmat, lo)
    b, bl = bb[:tb], bb[tb:]
    q_in = (q_ref[...] * jnp.exp(b)).astype(BF16)
    k = k_ref[...]
    k_in = (k * jnp.exp(-b)).astype(BF16)
    k_out = (k * jnp.exp(bl - b)).astype(BF16)
    ks = [slice(h * hk, (h + 1) * hk) for h in heads_r]
    vs = [slice(h * hv, (h + 1) * hv) for h in heads_r]

    r = lax.broadcasted_iota(jnp.int32, (tb, tb), 0)
    c = lax.broadcasted_iota(jnp.int32, (tb, tb), 1)
    causal = (c <= r) & ((r // chunk) == (c // chunk))
    scores = [_dot_nt(q_in[:, ks[h]], k_in[:, ks[h]]) for h in heads_r]
    atts = [jnp.where(causal, s, 0.0).astype(BF16) for s in scores]
    o_intra = [_dot(atts[h], v_ref[:, vs[h]]) for h in heads_r]

    sts = [st_ref[h] for h in heads_r]
    for ci in range(tb // chunk):
        rows = slice(ci * chunk, (ci + 1) * chunk)
        o_inter = [_dot_nt(q_in[rows, ks[h]], sts[h].astype(BF16)) for h in heads_r]
        updates = [_dot_tn(v_ref[rows, vs[h]], k_out[rows, ks[h]]) for h in heads_r]
        decay = jnp.exp(bl[ci * chunk:ci * chunk + 1])
        for h in heads_r:
            o_ref[rows, vs[h]] = o_intra[h][rows] + o_inter[h]
            sts[h] = sts[h] * decay[:, ks[h]] + updates[h]
    for h in heads_r:
        st_ref[h] = sts[h]

    @pl.when(step == pl.num_programs(1) - 1)
    def _():
        for h in heads_r:
            s_ref[0, h] = sts[h].T


def _gla_lmat(tb, chunk):
    r = np.arange(tb)[:, None]
    c = np.arange(tb)[None, :]
    same = (r // chunk) == (c // chunk)
    return jnp.asarray(np.concatenate([same & (c <= r), same], axis=0), dtype=BF16)


def _gla_rec(q, k, v, gk, *, batch, seq, heads, hk, hv):
    tb = min(GLA_BLOCK, seq)
    nl = seq // tb
    tok = lambda b, l: (b * nl + l, 0)
    dk, dv = heads * hk, heads * hv
    return pl.pallas_call(
        functools.partial(_gla_rec_kernel, tb=tb, chunk=min(GLA_CHUNK, seq)),
        grid=(batch, nl),
        in_specs=[pl.BlockSpec((tb, dk), tok), pl.BlockSpec((tb, dk), tok), pl.BlockSpec((tb, dv), tok),
                  pl.BlockSpec((tb, dk), tok), _const_spec((2 * tb, tb))],
        out_specs=[pl.BlockSpec((tb, dv), tok), pl.BlockSpec((1, heads, hk, hv), lambda b, l: (b, 0, 0, 0))],
        out_shape=[jax.ShapeDtypeStruct((batch * seq, dv), F32),
                   jax.ShapeDtypeStruct((batch, heads, hk, hv), F32)],
        scratch_shapes=[pltpu.VMEM((heads, hv, hk), F32)],
        compiler_params=_params("parallel", "arbitrary"),
        name="gla_rec",
    )(q, k, v, gk, _gla_lmat(tb, min(GLA_CHUNK, seq)))


def _gla_step_kernel(q_ref, k_ref, v_ref, gk_ref, s_ref, o_ref, so_ref, *, heads, hk, hv):
    eye = lax.broadcasted_iota(jnp.int32, (hk, hk), 0) == lax.broadcasted_iota(jnp.int32, (hk, hk), 1)

    def column(row):
        return jnp.sum(jnp.where(eye, jnp.broadcast_to(row, (hk, hk)), 0.0), axis=1, keepdims=True)

    for h in range(heads):
        ks = slice(h * hk, (h + 1) * hk)
        vs = slice(h * hv, (h + 1) * hv)
        s_new = s_ref[0, h] * column(jnp.exp(gk_ref[0, :, ks])) + column(k_ref[0, :, ks]) * v_ref[0, :, vs]
        so_ref[0, h] = s_new
        o_ref[0, :, vs] = jnp.sum(column(q_ref[0, :, ks]) * s_new, axis=0, keepdims=True)


def _gla_step(q, k, v, gk, states, layer, *, heads, hk, hv):
    n = q.shape[0]
    vec = lambda a: a.reshape(n, 1, a.shape[-1])
    spec = lambda w: pl.BlockSpec((1, 1, w), lambda b: (b, 0, 0))
    sspec = pl.BlockSpec((1, heads, hk, hv), lambda b: (b, 0, 0, 0))
    o, s = pl.pallas_call(
        functools.partial(_gla_step_kernel, heads=heads, hk=hk, hv=hv),
        grid=(n,),
        in_specs=[spec(heads * hk), spec(heads * hk), spec(heads * hv), spec(heads * hk),
                  pl.BlockSpec((None, 1, heads, hk, hv), lambda b: (layer, b, 0, 0, 0))],
        out_specs=[spec(heads * hv), sspec],
        out_shape=[jax.ShapeDtypeStruct((n, 1, heads * hv), F32), jax.ShapeDtypeStruct(states.shape[1:], F32)],
        compiler_params=_params("parallel"),
        name="gla_step",
    )(vec(q), vec(k), vec(v.astype(F32)), vec(gk), states)
    return o.reshape(n, heads * hv), s


def _mix_out_kernel(*refs, head_norm, emit_kv, final_norm, hv):
    refs = list(refs)
    o_ref, gate_ref, h_ref, p_ref, wout_ref, gple_ref, wgate_ref, wup_ref = refs[:8]
    pos = 8
    o = o_ref[...].astype(F32)
    if head_norm:
        gout = refs[pos][...]
        pos += 1
        segs = []
        for s in range(o.shape[1] // hv):
            seg = o[:, s * hv:(s + 1) * hv]
            segs.append(seg * lax.rsqrt(jnp.mean(seg * seg, axis=-1, keepdims=True) + EPS))
        o = jnp.concatenate(segs, axis=1) * gout
    gate = gate_ref[...]
    mixed = (o * (gate * _sigmoid(gate))).astype(BF16)
    h1 = h_ref[...] + _dot(mixed, wout_ref[...])
    ple_gate = _sigmoid(_dot(_rms(h1, gple_ref[...]).astype(BF16), wgate_ref[...]))
    h2 = h1 + ple_gate * _dot(p_ref[...].astype(BF16), wup_ref[...])
    if emit_kv:
        gkv_ref, wkv_ref = refs[pos:pos + 2]
        pos += 2
    if final_norm:
        gfin_ref = refs[pos]
        pos += 1
    y_ref = refs[pos]
    pos += 1
    y_ref[...] = _rms(h2, gfin_ref[...]) if final_norm else h2
    if emit_kv == "transposed":
        kf_ref, vf_ref, kb_ref, vb_ref = refs[pos:pos + 4]
        width = kf_ref.shape[1]
        hkv = _rms(h2, gkv_ref[...]).astype(BF16)
        kk = _dot_nt(wkv_ref[0:width, :], hkv)
        vv = _dot_nt(wkv_ref[width:2 * width, :], hkv)
        kf_ref[0] = kk
        vf_ref[0] = vv
        kb_ref[0] = kk.astype(BF16)
        vb_ref[0] = vv.astype(BF16)
    elif emit_kv == "rows":
        kf_ref, vf_ref = refs[pos:pos + 2]
        width = kf_ref.shape[1]
        hkv = _rms(h2, gkv_ref[...]).astype(BF16)
        kf_ref[...] = _dot_nt(hkv, wkv_ref[0:width, :])
        vf_ref[...] = _dot_nt(hkv, wkv_ref[width:2 * width, :])


def _mix_out(o, gate, h, p, layer, w_out, g_ple, w_gate, w_up, *, g_out=None, kv=None, g_final=None, hv=0, seq=1):
    t, d = h.shape
    tm = min(TOKEN_BLOCK // 2, t)
    row = lambda i: (i, 0)
    emit_kv = None if kv is None else ("transposed" if seq > 1 else "rows")
    args = [o, gate, h, p, w_out, g_ple, w_gate, w_up]
    in_specs = [pl.BlockSpec((tm, o.shape[1]), row), pl.BlockSpec((tm, d), row), pl.BlockSpec((tm, d), row),
                pl.BlockSpec((None, tm, p.shape[2]), lambda i: (layer, i, 0)), _const_spec(w_out.shape),
                _const_spec(g_ple.shape),
                _const_spec(w_gate.shape), _const_spec(w_up.shape)]
    if g_out is not None:
        args.append(g_out)
        in_specs.append(_const_spec(g_out.shape))
    if kv is not None:
        args.extend(kv)
        in_specs.extend([_const_spec(kv[0].shape), _const_spec(kv[1].shape)])
    if g_final is not None:
        args.append(g_final)
        in_specs.append(_const_spec(g_final.shape))
    out_specs = [pl.BlockSpec((tm, d), row)]
    out_shape = [jax.ShapeDtypeStruct((t, d), F32)]
    if emit_kv == "transposed":
        width = kv[1].shape[0] // 2
        blocks = seq // tm
        out_specs += [pl.BlockSpec((1, width, tm), lambda i: (i // blocks, 0, i % blocks))] * 4
        out_shape += ([jax.ShapeDtypeStruct((t // seq, width, seq), F32)] * 2
                      + [jax.ShapeDtypeStruct((t // seq, width, seq), BF16)] * 2)
    elif emit_kv == "rows":
        width = kv[1].shape[0] // 2
        out_specs += [pl.BlockSpec((tm, width), row)] * 2
        out_shape += [jax.ShapeDtypeStruct((t, width), F32)] * 2
    return pl.pallas_call(
        functools.partial(_mix_out_kernel, head_norm=g_out is not None, emit_kv=emit_kv,
                          final_norm=g_final is not None, hv=hv),
        grid=(t // tm,),
        in_specs=in_specs, out_specs=out_specs, out_shape=out_shape,
        compiler_params=_params("parallel"),
        name="mix_out",
    )(*args)


def _sb_in_kernel(x_ref, g_ref, w_ref, q_ref, gate_ref, *, width, scale):
    hn = _rms(x_ref[...], g_ref[...]).astype(BF16)
    q_ref[...] = (_dot(hn, w_ref[:, 0:width]) * scale).astype(BF16)
    gate_ref[...] = _dot(hn, w_ref[:, width:2 * width])


def _sb_in(x, g_pre, w_in, *, hd):
    t, d = x.shape
    width = w_in.shape[1] // 2
    tm = min(TOKEN_BLOCK, t)
    row = lambda i: (i, 0)
    return pl.pallas_call(
        functools.partial(_sb_in_kernel, width=width, scale=float(hd) ** -0.5 * LOG2E),
        grid=(t // tm,),
        in_specs=[pl.BlockSpec((tm, d), row), _const_spec(g_pre.shape), _const_spec(w_in.shape)],
        out_specs=[pl.BlockSpec((tm, width), row), pl.BlockSpec((tm, width), row)],
        out_shape=[jax.ShapeDtypeStruct((t, width), BF16), jax.ShapeDtypeStruct((t, width), F32)],
        compiler_params=_params("parallel"),
        name="sb_in",
    )(x, g_pre, w_in)


def _later_matrix(n):
    j = np.arange(n)[:, None]
    s = np.arange(n)[None, :]
    return jnp.asarray(j > s, dtype=BF16)


def _sb_tile_plan(nq, group):
    tiles = [(qi, qi - t, 1 if t == 0 else 0) for qi in range(nq) for t in range(qi + 1)]
    n_groups = 2 * -(-len(tiles) // (2 * group))
    tiles += [(0, 0, 2)] * ((n_groups + 1) * group - len(tiles))
    plan = np.asarray(tiles, dtype=np.int32)
    return n_groups, jnp.asarray(plan[:, 0]), jnp.asarray(plan[:, 1]), jnp.asarray(plan[:, 2])


def _sb_prompt_kernel(bias_ref, tq_ref, tk_ref, tm_ref, q_ref, kt_ref, vt_ref, u_ref, o_ref,
                      qm_ref, bm_ref, rsum_ref, logit_ref, acc_ref, sum_ref,
                      *, blk, hd, group, n_groups):
    pair = pl.program_id(1)
    heads_per_block = LANES // hd
    heads = range(heads_per_block)
    nq = q_ref.shape[0] // blk
    lane = lax.broadcasted_iota(jnp.int32, (1, LANES), 1)

    strictly_past = (lax.broadcasted_iota(jnp.int32, (blk, blk), 1) < lax.broadcasted_iota(jnp.int32, (blk, blk), 0))
    for h in heads:
        q = q_ref[...]
        qm_ref[h] = jnp.where((lane >= h * hd) & (lane < (h + 1) * hd), q, jnp.zeros_like(q))
        bias = bias_ref[pair * heads_per_block + h] * LOG2E
        bm_ref[h, 0] = jnp.full((blk, blk), bias, F32)
        bm_ref[h, 1] = jnp.where(strictly_past, bias, MASKED_LOG)
        bm_ref[h, 2] = jnp.full((blk, blk), MASKED_LOG, F32)
    acc_ref[...] = jnp.zeros_like(acc_ref)
    sum_ref[...] = jnp.zeros_like(sum_ref)

    def block(ref, j):
        return ref[0, :, pl.ds(pl.multiple_of(j * blk, blk), blk)]

    def tiles(g):
        return [g * group + u for u in range(group)]

    def stage_a_scores(g):
        zs = []
        for n in tiles(g):
            kt = block(kt_ref, tk_ref[n])
            rows = pl.ds(pl.multiple_of(tq_ref[n] * blk, blk), blk)
            zs.append([_dot(qm_ref[h, rows, :], kt) for h in heads])
        return zs

    def stage_a(g, slot):
        zs = stage_a_scores(g)
        for u, n in enumerate(tiles(g)):
            for h in heads:
                z = zs[u][h] + bm_ref[h, tm_ref[n]]
                neg_part = jnp.minimum(z, 0.0)
                neg_relu = neg_part - z
                softplus = jnp.log2(1.0 + jnp.exp2(neg_part + neg_relu))
                log_keep = neg_relu - softplus
                logit_ref[slot, u, h] = _dot(log_keep.astype(BF16), u_ref[...]) + (neg_part - softplus)
                rsum_ref[slot, u, h] = jnp.broadcast_to(jnp.sum(log_keep, axis=1, keepdims=True), (blk, LANES))

    def stage_c_values(g, slot):
        outs = []
        for u, n in enumerate(tiles(g)):
            qi = tq_ref[n]
            vt = block(vt_ref, tk_ref[n])
            for h in heads:
                later_sum = sum_ref[qi, h]
                a = jnp.exp2(logit_ref[slot, u, h] + jnp.concatenate([later_sum] * (blk // LANES), axis=1))
                outs.append(_dot_nt(a.astype(BF16), vt))
                sum_ref[qi, h] = later_sum + rsum_ref[slot, u, h]
        return outs

    def stage_c_accumulate(g, outs):
        for u, n in enumerate(tiles(g)):
            for h in heads:
                acc_ref[tq_ref[n], h] += outs[u * heads_per_block + h]

    def iteration(g, parity):
        outs = stage_c_values(g - 1, 1 - parity)
        stage_a(g, parity)
        stage_c_accumulate(g - 1, outs)

    stage_a(0, 0)

    def body(i, _):
        iteration(2 * i + 1, 1)
        iteration(2 * i + 2, 0)
        return 0

    lax.fori_loop(0, n_groups // 2, body, 0)

    for qi in range(nq):
        o = acc_ref[qi, heads_per_block - 1]
        for h in range(heads_per_block - 2, -1, -1):
            o = jnp.where(lane < (h + 1) * hd, acc_ref[qi, h], o)
        o_ref[qi * blk:(qi + 1) * blk, :] = o


def _sb_prompt(q, kt, vt, bias, *, batch, seq, hd):
    t, width = q.shape
    blk = min(SB_BLOCK, seq)
    nq = seq // blk
    group = SB_GROUP
    heads_per_block = LANES // hd
    n_groups, tile_q, tile_k, tile_mode = _sb_tile_plan(nq, group)
    smem = pl.BlockSpec(memory_space=pltpu.SMEM)
    qspec = pl.BlockSpec((seq, LANES), lambda b, p: (b, p))
    kvspec = pl.BlockSpec((1, LANES, seq), lambda b, p: (b, p, 0))
    tile = (group, heads_per_block, blk)
    return pl.pallas_call(
        functools.partial(_sb_prompt_kernel, blk=blk, hd=hd, group=group, n_groups=n_groups),
        grid=(batch, width // LANES),
        in_specs=[smem, smem, smem, smem, qspec, kvspec, kvspec, _const_spec((blk, blk))],
        out_specs=qspec,
        out_shape=jax.ShapeDtypeStruct((t, width), F32),
        scratch_shapes=[pltpu.VMEM((heads_per_block, seq, LANES), BF16),
                        pltpu.VMEM((heads_per_block, 3, blk, blk), F32),
                        pltpu.VMEM((2,) + tile + (LANES,), F32),
                        pltpu.VMEM((2,) + tile + (blk,), F32),
                        pltpu.VMEM((nq, heads_per_block, blk, LANES), F32),
                        pltpu.VMEM((nq, heads_per_block, blk, LANES), F32)],
        compiler_params=_params("parallel", "parallel"),
        name="sb_prompt",
    )(bias, tile_q, tile_k, tile_mode, q, kt, vt, _later_matrix(blk))


def _sb_decode_kernel(pt_ref, q_ref, bias_ref, u_ref, *refs, pages, heads, hd):
    del pt_ref
    k_refs = refs[:pages]
    v_refs = refs[pages:2 * pages]
    o_ref = refs[2 * pages]
    qb_ref, acc_ref, sum_ref, z_ref, a_ref = refs[2 * pages + 1:]
    step = pl.program_id(1)
    page = u_ref.shape[0]
    eye = lax.broadcasted_iota(jnp.int32, (hd, hd), 0) == lax.broadcasted_iota(jnp.int32, (hd, hd), 1)

    @pl.when(step == 0)
    def _():
        for h in range(heads):
            q_col = jnp.sum(jnp.where(eye, jnp.broadcast_to(q_ref[0, :, h * hd:(h + 1) * hd], (hd, hd)), 0.0),
                            axis=1, keepdims=True)
            qb_ref[h] = jnp.broadcast_to(q_col, (hd, page))
        acc_ref[...] = jnp.zeros_like(acc_ref)
        sum_ref[...] = jnp.zeros_like(sum_ref)

    for h in range(heads):
        qb = qb_ref[h]
        for j in range(pages):
            prod = k_refs[pages - 1 - j][0, h] * qb
            z_ref[pl.ds(j * heads + h, 1), :] = jnp.sum(prod, axis=0, keepdims=True)

    z = z_ref[...] + bias_ref[...] * LOG2E
    sp = jnp.log2(1.0 + jnp.exp2(-jnp.abs(z)))
    neg_part = jnp.minimum(z, 0.0)
    log_p = neg_part - sp
    log_keep = (neg_part - z) - sp
    hi, lo = _split_bf16(log_keep)
    u = u_ref[...]
    later_in = _dot(hi, u) + _dot(lo, u)
    row_sum = jnp.sum(log_keep, axis=1, keepdims=True)
    later_sum = sum_ref[...]
    for j in range(pages):
        rows = slice(j * heads, (j + 1) * heads)
        a_ref[rows, :] = jnp.exp2(log_p[rows] + later_in[rows] + later_sum)
        later_sum = later_sum + row_sum[rows]
    sum_ref[...] = later_sum

    for h in range(heads):
        acc = acc_ref[h]
        for j in range(pages):
            a_row = a_ref[pl.ds(j * heads + h, 1), :]
            acc = acc + jnp.broadcast_to(a_row, (hd, page)) * v_refs[pages - 1 - j][0, h]
        acc_ref[h] = acc

    @pl.when(step == pl.num_programs(1) - 1)
    def _():
        for h in range(heads):
            o_col = jnp.sum(acc_ref[h], axis=-1, keepdims=True)
            o_ref[0, :, h * hd:(h + 1) * hd] = jnp.sum(jnp.where(eye, jnp.broadcast_to(o_col, (hd, hd)), 0.0),
                                                      axis=0, keepdims=True)


def _sb_decode(q, cache_k, cache_v, page_table, bias, *, heads, hd):
    n, width = q.shape
    page = cache_k.shape[1]
    n_pages = page_table.shape[1]
    pages = min(DECODE_PAGES, n_pages)
    steps = n_pages // pages
    ck = jnp.transpose(cache_k, (0, 2, 3, 1))
    cv = jnp.transpose(cache_v, (0, 2, 3, 1))

    def page_spec(i):
        return pl.BlockSpec((1, heads, hd, page),
                            lambda b, s, pt: (pt[b, (steps - 1 - s) * pages + i], 0, 0, 0))

    row_spec = pl.BlockSpec((1, 1, width), lambda b, s, pt: (b, 0, 0))
    grid_spec = pltpu.PrefetchScalarGridSpec(
        num_scalar_prefetch=1,
        grid=(n, steps),
        in_specs=[row_spec, pl.BlockSpec((pages * heads, 1), lambda b, s, pt: (0, 0)),
                  pl.BlockSpec((page, page), lambda b, s, pt: (0, 0))]
        + [page_spec(i) for i in range(pages)] * 2,
        out_specs=row_spec,
        scratch_shapes=[pltpu.VMEM((heads, hd, page), F32), pltpu.VMEM((heads, hd, page), F32),
                        pltpu.VMEM((heads, 1), F32), pltpu.VMEM((pages * heads, page), F32),
                        pltpu.VMEM((pages * heads, page), F32)],
    )
    o = pl.pallas_call(
        functools.partial(_sb_decode_kernel, pages=pages, heads=heads, hd=hd),
        grid_spec=grid_spec,
        out_shape=jax.ShapeDtypeStruct((n, 1, width), F32),
        compiler_params=_params("parallel", "arbitrary"),
        name="sb_decode",
    )(page_table, q.reshape(n, 1, width), jnp.tile(bias, pages).reshape(pages * heads, 1),
      _later_matrix(page), *([ck] * pages), *([cv] * pages))
    return o.reshape(n, width)


def _run_group(x, p, w, *, batch, seq, gla_state, past):
    depth = p.shape[0]
    n_a = w["gla_w_in"].shape[0]
    heads, hk, hv = w["gla_dims"]
    sb_heads, hd = w["sb_dims"]
    dk, dv = heads * hk, heads * hv
    t = batch * seq
    h = x.reshape(t, x.shape[-1])
    p = p.reshape(depth, t, p.shape[-1])
    row = lambda a: a.reshape(1, -1)
    states = []
    k_new = v_new = k_b = v_b = None
    for i in range(depth):
        ple = (row(w["g_ple"][i]), w["w_ple_gate"][i], w["w_ple_up"][i])
        g_final = row(w["g_final"]) if i == depth - 1 else None
        if i < n_a:
            q, k, v, gate, gk = _gla_in(h, row(w["g_pre"][i]), w["gla_w_main"][i], w["gla_w_lr"][i],
                                        w["gla_w_gk2"][i], row(w["gla_b_gk"][i]), dk=dk, dv=dv, hk=hk)
            if gla_state is None:
                o, s = _gla_rec(q, k, v, gk, batch=batch, seq=seq, heads=heads, hk=hk, hv=hv)
            else:
                o, s = _gla_step(q, k, v, gk, gla_state, i, heads=heads, hk=hk, hv=hv)
            states.append(s)
            kv = (row(w["g_kv"]), w["w_kv_t"]) if i == n_a - 1 else None
            res = _mix_out(o, gate, h, p, i, w["gla_w_out"][i], *ple, g_out=row(jnp.tile(w["gla_g_out"][i], heads)),
                           kv=kv, g_final=g_final, hv=hv, seq=seq)
            h = res[0]
            if kv is not None:
                k_new, v_new = res[1:3]
                if seq > 1:
                    k_b, v_b = res[3:5]
        else:
            j = i - n_a
            q, gate = _sb_in(h, row(w["g_pre"][i]), w["sb_w_in"][j], hd=hd)
            if past is None:
                o = _sb_prompt(q, k_b, v_b, w["sb_bias"][j], batch=batch, seq=seq, hd=hd)
            else:
                o = _sb_decode(q.astype(F32), *past, w["sb_bias"][j], heads=sb_heads, hd=hd)
            h = _mix_out(o, gate, h, p, i, w["sb_w_out"][j], *ple, g_final=g_final)[0]
    return h, jnp.stack(states), k_new, v_new


def kernel(x_prompt, x_sample, state_gla, cache_k, cache_v, page_table, p_prompt, p_sample, g_pre, w_ple_gate,
           w_ple_up, g_ple, gla_w_in, gla_w_gk2, gla_b_gk, gla_g_out, gla_w_out, g_kv, w_kv, sb_w_in, sb_bias,
           sb_w_out, g_final):
    batch, seq, d = x_prompt.shape
    n_dec, dec_seq, _ = x_sample.shape
    assert dec_seq == 1
    _, _, heads, hk, hv = state_gla.shape
    sb_heads, hd = cache_k.shape[2:]
    dk, dv = heads * hk, heads * hv
    rank = gla_w_gk2.shape[1]
    main = 2 * dk + 2 * dv
    bf = lambda a: a.astype(BF16)
    w = dict(
        g_pre=g_pre, g_ple=g_ple, g_kv=g_kv, g_final=g_final, gla_b_gk=gla_b_gk, gla_g_out=gla_g_out,
        sb_bias=sb_bias, gla_w_in=gla_w_in,
        w_ple_gate=bf(w_ple_gate), w_ple_up=bf(w_ple_up), gla_w_out=bf(gla_w_out), w_kv_t=bf(w_kv.T),
        sb_w_in=bf(sb_w_in), sb_w_out=bf(sb_w_out),
        gla_w_main=bf(gla_w_in[:, :, :main]),
        gla_w_lr=bf(jnp.pad(gla_w_in[:, :, main:], ((0, 0), (0, 0), (0, LANES - rank)))),
        gla_w_gk2=bf(jnp.pad(gla_w_gk2, ((0, 0), (0, LANES - rank), (0, 0)))),
        gla_dims=(heads, hk, hv), sb_dims=(sb_heads, hd),
    )
    y_p, s_p, k_p, v_p = _run_group(x_prompt, p_prompt, w, batch=batch, seq=seq, gla_state=None, past=None)
    y_s, s_s, k_s, v_s = _run_group(x_sample, p_sample, w, batch=n_dec, seq=1, gla_state=state_gla,
                                    past=(cache_k, cache_v, page_table))
    from_t = lambda a: jnp.transpose(a.reshape(batch, sb_heads, hd, seq), (0, 3, 1, 2))
    from_rows = lambda a: a.reshape(n_dec, 1, sb_heads, hd)
    return (y_p.reshape(batch, seq, d), y_s.reshape(n_dec, 1, d), s_p, s_s,
            from_t(k_p), from_t(v_p), from_rows(k_s), from_rows(v_s))
```

```python
import functools

import numpy as np
import jax
import jax.numpy as jnp
from jax import lax
from jax.experimental import pallas as pl
from jax.experimental.pallas import tpu as pltpu

F32 = jnp.float32
BF16 = jnp.bfloat16

EPS = 1e-6
GATE_NORM = 16.0
GLA_CHUNK = 64
LANES = 128
VMEM_LIMIT = 48 * 1024 * 1024

TOKEN_BLOCK = 512
GLA_BLOCK = 256
SB_BLOCK = 256
SB_GROUP = 3
LOG2E = 1.4426950408889634
MASKED_LOG = -1e30
DECODE_PAGES = 16


def _dot(a, b):
    return jnp.dot(a, b, preferred_element_type=F32)


def _dot_nt(a, b):
    return lax.dot_general(a, b, (((1,), (1,)), ((), ())), preferred_element_type=F32)


def _dot_tn(a, b):
    return lax.dot_general(a, b, (((0,), (0,)), ((), ())), preferred_element_type=F32)


def _split_bf16(x):
    hi = x.astype(BF16)
    lo = (x - hi.astype(F32)).astype(BF16)
    return hi, lo


def _rms(x, g):
    ms = jnp.mean(x * x, axis=-1, keepdims=True)
    return x * lax.rsqrt(ms + EPS) * g


def _softplus_neg_abs(z):
    return jnp.log(1.0 + jnp.exp(-jnp.abs(z)))


def _log_sigmoid(z):
    return jnp.minimum(z, 0.0) - _softplus_neg_abs(z)


def _sigmoid(z):
    return 1.0 / (1.0 + jnp.exp(-z))


def _params(*sem):
    return pltpu.CompilerParams(dimension_semantics=sem, vmem_limit_bytes=VMEM_LIMIT)


def _const_spec(shape):
    nd = len(shape)
    return pl.BlockSpec(shape, lambda *_: (0,) * nd)


def _gla_in_kernel(x_ref, g_ref, w_ref, wlr_ref, wgk2_ref, bgk_ref,
                   q_ref, k_ref, v_ref, gate_ref, gk_ref, *, dk, dv, qscale):
    hn = _rms(x_ref[...], g_ref[...]).astype(BF16)
    q_ref[...] = _dot(hn, w_ref[:, 0:dk]) * qscale
    k_ref[...] = _dot(hn, w_ref[:, dk:2 * dk])
    v_ref[...] = _dot(hn, w_ref[:, 2 * dk:2 * dk + dv]).astype(BF16)
    gate_ref[...] = _dot(hn, w_ref[:, 2 * dk + dv:2 * dk + 2 * dv])
    lr = _dot(hn, wlr_ref[...]).astype(BF16)
    gk_ref[...] = _log_sigmoid(_dot(lr, wgk2_ref[...]) + bgk_ref[...]) / GATE_NORM


def _gla_in(x, g_pre, w_main, w_lr, w_gk2, b_gk, *, dk, dv, hk):
    t, d = x.shape
    tm = min(TOKEN_BLOCK, t)
    row = lambda i: (i, 0)
    return pl.pallas_call(
        functools.partial(_gla_in_kernel, dk=dk, dv=dv, qscale=float(hk) ** -0.5),
        grid=(t // tm,),
        in_specs=[pl.BlockSpec((tm, d), row), _const_spec(g_pre.shape), _const_spec(w_main.shape),
                  _const_spec(w_lr.shape), _const_spec(w_gk2.shape), _const_spec(b_gk.shape)],
        out_specs=[pl.BlockSpec((tm, dk), row), pl.BlockSpec((tm, dk), row), pl.BlockSpec((tm, dv), row),
                   pl.BlockSpec((tm, dv), row), pl.BlockSpec((tm, dk), row)],
        out_shape=[jax.ShapeDtypeStruct((t, dk), F32), jax.ShapeDtypeStruct((t, dk), F32),
                   jax.ShapeDtypeStruct((t, dv), BF16), jax.ShapeDtypeStruct((t, dv), F32),
                   jax.ShapeDtypeStruct((t, dk), F32)],
        compiler_params=_params("parallel"),
        name="gla_in",
    )(x, g_pre, w_main, w_lr, w_gk2, b_gk)


def _gla_rec_kernel(q_ref, k_ref, v_ref, gk_ref, lmat_ref, o_ref, s_ref, st_ref, *, tb, chunk):
    step = pl.program_id(1)
    heads, hv, hk = st_ref.shape
    heads_r = range(heads)

    @pl.when(step == 0)
    def _():
        st_ref[...] = jnp.zeros_like(st_ref)

    hi, lo = _split_bf16(gk_ref[...])
    lmat = lmat_ref[...]
    bb = _dot(lmat, hi) + _dot(lmat, lo)
    b, bl = bb[:tb], bb[tb:]
    q_in = (q_ref[...] * jnp.exp(b)).astype(BF16)
    k = k_ref[...]
    k_in = (k * jnp.exp(-b)).astype(BF16)
    k_out = (k * jnp.exp(bl - b)).astype(BF16)
    ks = [slice(h * hk, (h + 1) * hk) for h in heads_r]
    vs = [slice(h * hv, (h + 1) * hv) for h in heads_r]

    r = lax.broadcasted_iota(jnp.int32, (tb, tb), 0)
    c = lax.broadcasted_iota(jnp.int32, (tb, tb), 1)
    causal = (c <= r) & ((r // chunk) == (c // chunk))
    scores = [_dot_nt(q_in[:, ks[h]], k_in[:, ks[h]]) for h in heads_r]
    atts = [jnp.where(causal, s, 0.0).astype(BF16) for s in scores]
    o_intra = [_dot(atts[h], v_ref[:, vs[h]]) for h in heads_r]

    sts = [st_ref[h] for h in heads_r]
    for ci in range(tb // chunk):
        rows = slice(ci * chunk, (ci + 1) * chunk)
        o_inter = [_dot_nt(q_in[rows, ks[h]], sts[h].astype(BF16)) for h in heads_r]
        updates = [_dot_tn(v_ref[rows, vs[h]], k_out[rows, ks[h]]) for h in heads_r]
        decay = jnp.exp(bl[ci * chunk:ci * chunk + 1])
        for h in heads_r:
            o_ref[rows, vs[h]] = o_intra[h][rows] + o_inter[h]
            sts[h] = sts[h] * decay[:, ks[h]] + updates[h]
    for h in heads_r:
        st_ref[h] = sts[h]

    @pl.when(step == pl.num_programs(1) - 1)
    def _():
        for h in heads_r:
            s_ref[0, h] = sts[h].T


def _gla_lmat(tb, chunk):
    r = np.arange(tb)[:, None]
    c = np.arange(tb)[None, :]
    same = (r // chunk) == (c // chunk)
    return jnp.asarray(np.concatenate([same & (c <= r), same], axis=0), dtype=BF16)


def _gla_rec(q, k, v, gk, *, batch, seq, heads, hk, hv):
    tb = min(GLA_BLOCK, seq)
    nl = seq // tb
    tok = lambda b, l: (b * nl + l, 0)
    dk, dv = heads * hk, heads * hv
    return pl.pallas_call(
        functools.partial(_gla_rec_kernel, tb=tb, chunk=min(GLA_CHUNK, seq)),
        grid=(batch, nl),
        in_specs=[pl.BlockSpec((tb, dk), tok), pl.BlockSpec((tb, dk), tok), pl.BlockSpec((tb, dv), tok),
                  pl.BlockSpec((tb, dk), tok), _const_spec((2 * tb, tb))],
        out_specs=[pl.BlockSpec((tb, dv), tok), pl.BlockSpec((1, heads, hk, hv), lambda b, l: (b, 0, 0, 0))],
        out_shape=[jax.ShapeDtypeStruct((batch * seq, dv), F32),
                   jax.ShapeDtypeStruct((batch, heads, hk, hv), F32)],
        scratch_shapes=[pltpu.VMEM((heads, hv, hk), F32)],
        compiler_params=_params("parallel", "arbitrary"),
        name="gla_rec",
    )(q, k, v, gk, _gla_lmat(tb, min(GLA_CHUNK, seq)))


def _gla_step_kernel(q_ref, k_ref, v_ref, gk_ref, s_ref, o_ref, so_ref, *, heads, hk, hv):
    eye = lax.broadcasted_iota(jnp.int32, (hk, hk), 0) == lax.broadcasted_iota(jnp.int32, (hk, hk), 1)

    def column(row):
        return jnp.sum(jnp.where(eye, jnp.broadcast_to(row, (hk, hk)), 0.0), axis=1, keepdims=True)

    for h in range(heads):
        ks = slice(h * hk, (h + 1) * hk)
        vs = slice(h * hv, (h + 1) * hv)
        s_new = s_ref[0, h] * column(jnp.exp(gk_ref[0, :, ks])) + column(k_ref[0, :, ks]) * v_ref[0, :, vs]
        so_ref[0, h] = s_new
        o_ref[0, :, vs] = jnp.sum(column(q_ref[0, :, ks]) * s_new, axis=0, keepdims=True)


def _gla_step(q, k, v, gk, states, layer, *, heads, hk, hv):
    n = q.shape[0]
    vec = lambda a: a.reshape(n, 1, a.shape[-1])
    spec = lambda w: pl.BlockSpec((1, 1, w), lambda b: (b, 0, 0))
    sspec = pl.BlockSpec((1, heads, hk, hv), lambda b: (b, 0, 0, 0))
    o, s = pl.pallas_call(
        functools.partial(_gla_step_kernel, heads=heads, hk=hk, hv=hv),
        grid=(n,),
        in_specs=[spec(heads * hk), spec(heads * hk), spec(heads * hv), spec(heads * hk),
                  pl.BlockSpec((None, 1, heads, hk, hv), lambda b: (layer, b, 0, 0, 0))],
        out_specs=[spec(heads * hv), sspec],
        out_shape=[jax.ShapeDtypeStruct((n, 1, heads * hv), F32), jax.ShapeDtypeStruct(states.shape[1:], F32)],
        compiler_params=_params("parallel"),
        name="gla_step",
    )(vec(q), vec(k), vec(v.astype(F32)), vec(gk), states)
    return o.reshape(n, heads * hv), s


def _mix_out_kernel(*refs, head_norm, emit_kv, final_norm, hv):
    refs = list(refs)
    o_ref, gate_ref, h_ref, p_ref, wout_ref, gple_ref, wgate_ref, wup_ref = refs[:8]
    pos = 8
    o = o_ref[...].astype(F32)
    if head_norm:
        gout = refs[pos][...]
        pos += 1
        segs = []
        for s in range(o.shape[1] // hv):
            seg = o[:, s * hv:(s + 1) * hv]
            segs.append(seg * lax.rsqrt(jnp.mean(seg * seg, axis=-1, keepdims=True) + EPS))
        o = jnp.concatenate(segs, axis=1) * gout
    gate = gate_ref[...]
    mixed = (o * (gate * _sigmoid(gate))).astype(BF16)
    h1 = h_ref[...] + _dot(mixed, wout_ref[...])
    ple_gate = _sigmoid(_dot(_rms(h1, gple_ref[...]).astype(BF16), wgate_ref[...]))
    h2 = h1 + ple_gate * _dot(p_ref[...].astype(BF16), wup_ref[...])
    if emit_kv:
        gkv_ref, wkv_ref = refs[pos:pos + 2]
        pos += 2
    if final_norm:
        gfin_ref = refs[pos]
        pos += 1
    y_ref = refs[pos]
    pos += 1
    y_ref[...] = _rms(h2, gfin_ref[...]) if final_norm else h2
    if emit_kv == "transposed":
        kf_ref, vf_ref, kb_ref, vb_ref = refs[pos:pos + 4]
        width = kf_ref.shape[1]
        hkv = _rms(h2, gkv_ref[...]).astype(BF16)
        kk = _dot_nt(wkv_ref[0:width, :], hkv)
        vv = _dot_nt(wkv_ref[width:2 * width, :], hkv)
        kf_ref[0] = kk
        vf_ref[0] = vv
        kb_ref[0] = kk.astype(BF16)
        vb_ref[0] = vv.astype(BF16)
    elif emit_kv == "rows":
        kf_ref, vf_ref = refs[pos:pos + 2]
        width = kf_ref.shape[1]
        hkv = _rms(h2, gkv_ref[...]).astype(BF16)
        kf_ref[...] = _dot_nt(hkv, wkv_ref[0:width, :])
        vf_ref[...] = _dot_nt(hkv, wkv_ref[width:2 * width, :])


def _mix_out(o, gate, h, p, layer, w_out, g_ple, w_gate, w_up, *, g_out=None, kv=None, g_final=None, hv=0, seq=1):
    t, d = h.shape
    tm = min(TOKEN_BLOCK, t)
    row = lambda i: (i, 0)
    emit_kv = None if kv is None else ("transposed" if seq > 1 else "rows")
    args = [o, gate, h, p, w_out, g_ple, w_gate, w_up]
    in_specs = [pl.BlockSpec((tm, o.shape[1]), row), pl.BlockSpec((tm, d), row), pl.BlockSpec((tm, d), row),
                pl.BlockSpec((None, tm, p.shape[2]), lambda i: (layer, i, 0)), _const_spec(w_out.shape),
                _const_spec(g_ple.shape),
                _const_spec(w_gate.shape), _const_spec(w_up.shape)]
    if g_out is not None:
        args.append(g_out)
        in_specs.append(_const_spec(g_out.shape))
    if kv is not None:
        args.extend(kv)
        in_specs.extend([_const_spec(kv[0].shape), _const_spec(kv[1].shape)])
    if g_final is not None:
        args.append(g_final)
        in_specs.append(_const_spec(g_final.shape))
    out_specs = [pl.BlockSpec((tm, d), row)]
    out_shape = [jax.ShapeDtypeStruct((t, d), F32)]
    if emit_kv == "transposed":
        width = kv[1].shape[0] // 2
        blocks = seq // tm
        out_specs += [pl.BlockSpec((1, width, tm), lambda i: (i // blocks, 0, i % blocks))] * 4
        out_shape += ([jax.ShapeDtypeStruct((t // seq, width, seq), F32)] * 2
                      + [jax.ShapeDtypeStruct((t // seq, width, seq), BF16)] * 2)
    elif emit_kv == "rows":
        width = kv[1].shape[0] // 2
        out_specs += [pl.BlockSpec((tm, width), row)] * 2
        out_shape += [jax.ShapeDtypeStruct((t, width), F32)] * 2
    return pl.pallas_call(
        functools.partial(_mix_out_kernel, head_norm=g_out is not None, emit_kv=emit_kv,
                          final_norm=g_final is not None, hv=hv),
        grid=(t // tm,),
        in_specs=in_specs, out_specs=out_specs, out_shape=out_shape,
        compiler_params=_params("parallel"),
        name="mix_out",
    )(*args)


def _sb_in_kernel(x_ref, g_ref, w_ref, q_ref, gate_ref, *, width, scale):
    hn = _rms(x_ref[...], g_ref[...]).astype(BF16)
    q_ref[...] = (_dot(hn, w_ref[:, 0:width]) * scale).astype(BF16)
    gate_ref[...] = _dot(hn, w_ref[:, width:2 * width])


def _sb_in(x, g_pre, w_in, *, hd):
    t, d = x.shape
    width = w_in.shape[1] // 2
    tm = min(TOKEN_BLOCK, t)
    row = lambda i: (i, 0)
    return pl.pallas_call(
        functools.partial(_sb_in_kernel, width=width, scale=float(hd) ** -0.5 * LOG2E),
        grid=(t // tm,),
        in_specs=[pl.BlockSpec((tm, d), row), _const_spec(g_pre.shape), _const_spec(w_in.shape)],
        out_specs=[pl.BlockSpec((tm, width), row), pl.BlockSpec((tm, width), row)],
        out_shape=[jax.ShapeDtypeStruct((t, width), BF16), jax.ShapeDtypeStruct((t, width), F32)],
        compiler_params=_params("parallel"),
        name="sb_in",
    )(x, g_pre, w_in)


def _later_matrix(n):
    j = np.arange(n)[:, None]
    s = np.arange(n)[None, :]
    return jnp.asarray(j > s, dtype=BF16)


def _sb_tile_plan(nq, group):
    tiles = [(qi, qi - t, 1 if t == 0 else 0) for qi in range(nq) for t in range(qi + 1)]
    n_groups = 2 * -(-len(tiles) // (2 * group))
    tiles += [(0, 0, 2)] * ((n_groups + 1) * group - len(tiles))
    plan = np.asarray(tiles, dtype=np.int32)
    return n_groups, jnp.asarray(plan[:, 0]), jnp.asarray(plan[:, 1]), jnp.asarray(plan[:, 2])


def _sb_prompt_kernel(bias_ref, tq_ref, tk_ref, tm_ref, q_ref, kt_ref, vt_ref, u_ref, o_ref,
                      qm_ref, bm_ref, rsum_ref, logit_ref, acc_ref, sum_ref,
                      *, blk, hd, group, n_groups):
    pair = pl.program_id(1)
    heads_per_block = LANES // hd
    heads = range(heads_per_block)
    nq = q_ref.shape[0] // blk
    lane = lax.broadcasted_iota(jnp.int32, (1, LANES), 1)

    strictly_past = (lax.broadcasted_iota(jnp.int32, (blk, blk), 1) < lax.broadcasted_iota(jnp.int32, (blk, blk), 0))
    for h in heads:
        q = q_ref[...]
        qm_ref[h] = jnp.where((lane >= h * hd) & (lane < (h + 1) * hd), q, jnp.zeros_like(q))
        bias = bias_ref[pair * heads_per_block + h] * LOG2E
        bm_ref[h, 0] = jnp.full((blk, blk), bias, F32)
        bm_ref[h, 1] = jnp.where(strictly_past, bias, MASKED_LOG)
        bm_ref[h, 2] = jnp.full((blk, blk), MASKED_LOG, F32)
    acc_ref[...] = jnp.zeros_like(acc_ref)
    sum_ref[...] = jnp.zeros_like(sum_ref)

    def block(ref, j):
        return ref[0, :, pl.ds(pl.multiple_of(j * blk, blk), blk)]

    def tiles(g):
        return [g * group + u for u in range(group)]

    def stage_a_scores(g):
        zs = []
        for n in tiles(g):
            kt = block(kt_ref, tk_ref[n])
            rows = pl.ds(pl.multiple_of(tq_ref[n] * blk, blk), blk)
            zs.append([_dot(qm_ref[h, rows, :], kt) for h in heads])
        return zs

    def stage_a(g, slot):
        zs = stage_a_scores(g)
        for u, n in enumerate(tiles(g)):
            for h in heads:
                z = zs[u][h] + bm_ref[h, tm_ref[n]]
                neg_part = jnp.minimum(z, 0.0)
                neg_relu = neg_part - z
                softplus = jnp.log2(1.0 + jnp.exp2(neg_part + neg_relu))
                log_keep = neg_relu - softplus
                logit_ref[slot, u, h] = _dot(log_keep.astype(BF16), u_ref[...]) + (neg_part - softplus)
                rsum_ref[slot, u, h] = jnp.broadcast_to(jnp.sum(log_keep, axis=1, keepdims=True), (blk, LANES))

    def stage_c_values(g, slot):
        outs = []
        for u, n in enumerate(tiles(g)):
            qi = tq_ref[n]
            vt = block(vt_ref, tk_ref[n])
            for h in heads:
                later_sum = sum_ref[qi, h]
                a = jnp.exp2(logit_ref[slot, u, h] + jnp.concatenate([later_sum] * (blk // LANES), axis=1))
                outs.append(_dot_nt(a.astype(BF16), vt))
                sum_ref[qi, h] = later_sum + rsum_ref[slot, u, h]
        return outs

    def stage_c_accumulate(g, outs):
        for u, n in enumerate(tiles(g)):
            for h in heads:
                acc_ref[tq_ref[n], h] += outs[u * heads_per_block + h]

    def iteration(g, parity):
        outs = stage_c_values(g - 1, 1 - parity)
        stage_a(g, parity)
        stage_c_accumulate(g - 1, outs)

    stage_a(0, 0)

    def body(i, _):
        iteration(2 * i + 1, 1)
        iteration(2 * i + 2, 0)
        return 0

    lax.fori_loop(0, n_groups // 2, body, 0)

    for qi in range(nq):
        o = acc_ref[qi, heads_per_block - 1]
        for h in range(heads_per_block - 2, -1, -1):
            o = jnp.where(lane < (h + 1) * hd, acc_ref[qi, h], o)
        o_ref[qi * blk:(qi + 1) * blk, :] = o


def _sb_prompt(q, kt, vt, bias, *, batch, seq, hd):
    t, width = q.shape
    blk = min(SB_BLOCK, seq)
    nq = seq // blk
    group = SB_GROUP
    heads_per_block = LANES // hd
    n_groups, tile_q, tile_k, tile_mode = _sb_tile_plan(nq, group)
    smem = pl.BlockSpec(memory_space=pltpu.SMEM)
    qspec = pl.BlockSpec((seq, LANES), lambda b, p: (b, p))
    kvspec = pl.BlockSpec((1, LANES, seq), lambda b, p: (b, p, 0))
    tile = (group, heads_per_block, blk)
    return pl.pallas_call(
        functools.partial(_sb_prompt_kernel, blk=blk, hd=hd, group=group, n_groups=n_groups),
        grid=(batch, width // LANES),
        in_specs=[smem, smem, smem, smem, qspec, kvspec, kvspec, _const_spec((blk, blk))],
        out_specs=qspec,
        out_shape=jax.ShapeDtypeStruct((t, width), F32),
        scratch_shapes=[pltpu.VMEM((heads_per_block, seq, LANES), BF16),
                        pltpu.VMEM((heads_per_block, 3, blk, blk), F32),
                        pltpu.VMEM((2,) + tile + (LANES,), F32),
                        pltpu.VMEM((2,) + tile + (blk,), F32),
                        pltpu.VMEM((nq, heads_per_block, blk, LANES), F32),
                        pltpu.VMEM((nq, heads_per_block, blk, LANES), F32)],
        compiler_params=_params("parallel", "parallel"),
        name="sb_prompt",
    )(bias, tile_q, tile_k, tile_mode, q, kt, vt, _later_matrix(blk))


def _sb_decode_kernel(pt_ref, q_ref, bias_ref, u_ref, *refs, pages, heads, hd):
    del pt_ref
    k_refs = refs[:pages]
    v_refs = refs[pages:2 * pages]
    o_ref = refs[2 * pages]
    qb_ref, acc_ref, sum_ref, z_ref, a_ref = refs[2 * pages + 1:]
    step = pl.program_id(1)
    page = u_ref.shape[0]
    eye = lax.broadcasted_iota(jnp.int32, (hd, hd), 0) == lax.broadcasted_iota(jnp.int32, (hd, hd), 1)

    @pl.when(step == 0)
    def _():
        for h in range(heads):
            q_col = jnp.sum(jnp.where(eye, jnp.broadcast_to(q_ref[0, :, h * hd:(h + 1) * hd], (hd, hd)), 0.0),
                            axis=1, keepdims=True)
            qb_ref[h] = jnp.broadcast_to(q_col, (hd, page))
        acc_ref[...] = jnp.zeros_like(acc_ref)
        sum_ref[...] = jnp.zeros_like(sum_ref)

    for h in range(heads):
        qb = qb_ref[h]
        for j in range(pages):
            prod = k_refs[pages - 1 - j][0, h] * qb
            z_ref[pl.ds(j * heads + h, 1), :] = jnp.sum(prod, axis=0, keepdims=True)

    z = z_ref[...] + bias_ref[...] * LOG2E
    sp = jnp.log2(1.0 + jnp.exp2(-jnp.abs(z)))
    neg_part = jnp.minimum(z, 0.0)
    log_p = neg_part - sp
    log_keep = (neg_part - z) - sp
    hi, lo = _split_bf16(log_keep)
    u = u_ref[...]
    later_in = _dot(hi, u) + _dot(lo, u)
    row_sum = jnp.sum(log_keep, axis=1, keepdims=True)
    later_sum = sum_ref[...]
    for j in range(pages):
        rows = slice(j * heads, (j + 1) * heads)
        a_ref[rows, :] = jnp.exp2(log_p[rows] + later_in[rows] + later_sum)
        later_sum = later_sum + row_sum[rows]
    sum_ref[...] = later_sum

    for h in range(heads):
        acc = acc_ref[h]
        for j in range(pages):
            a_row = a_ref[pl.ds(j * heads + h, 1), :]
            acc = acc + jnp.broadcast_to(a_row, (hd, page)) * v_refs[pages - 1 - j][0, h]
        acc_ref[h] = acc

    @pl.when(step == pl.num_programs(1) - 1)
    def _():
        for h in range(heads):
            o_col = jnp.sum(acc_ref[h], axis=-1, keepdims=True)
            o_ref[0, :, h * hd:(h + 1) * hd] = jnp.sum(jnp.where(eye, jnp.broadcast_to(o_col, (hd, hd)), 0.0),
                                                      axis=0, keepdims=True)


def _sb_decode(q, cache_k, cache_v, page_table, bias, *, heads, hd):
    n, width = q.shape
    page = cache_k.shape[1]
    n_pages = page_table.shape[1]
    pages = min(DECODE_PAGES, n_pages)
    steps = n_pages // pages
    ck = jnp.transpose(cache_k, (0, 2, 3, 1))
    cv = jnp.transpose(cache_v, (0, 2, 3, 1))

    def page_spec(i):
        return pl.BlockSpec((1, heads, hd, page),
                            lambda b, s, pt: (pt[b, (steps - 1 - s) * pages + i], 0, 0, 0))

    row_spec = pl.BlockSpec((1, 1, width), lambda b, s, pt: (b, 0, 0))
    grid_spec = pltpu.PrefetchScalarGridSpec(
        num_scalar_prefetch=1,
        grid=(n, steps),
        in_specs=[row_spec, pl.BlockSpec((pages * heads, 1), lambda b, s, pt: (0, 0)),
                  pl.BlockSpec((page, page), lambda b, s, pt: (0, 0))]
        + [page_spec(i) for i in range(pages)] * 2,
        out_specs=row_spec,
        scratch_shapes=[pltpu.VMEM((heads, hd, page), F32), pltpu.VMEM((heads, hd, page), F32),
                        pltpu.VMEM((heads, 1), F32), pltpu.VMEM((pages * heads, page), F32),
                        pltpu.VMEM((pages * heads, page), F32)],
    )
    o = pl.pallas_call(
        functools.partial(_sb_decode_kernel, pages=pages, heads=heads, hd=hd),
        grid_spec=grid_spec,
        out_shape=jax.ShapeDtypeStruct((n, 1, width), F32),
        compiler_params=_params("parallel", "arbitrary"),
        name="sb_decode",
    )(page_table, q.reshape(n, 1, width), jnp.tile(bias, pages).reshape(pages * heads, 1),
      _later_matrix(page), *([ck] * pages), *([cv] * pages))
    return o.reshape(n, width)


def _run_group(x, p, w, *, batch, seq, gla_state, past):
    depth = p.shape[0]
    n_a = w["gla_w_in"].shape[0]
    heads, hk, hv = w["gla_dims"]
    sb_heads, hd = w["sb_dims"]
    dk, dv = heads * hk, heads * hv
    t = batch * seq
    h = x.reshape(t, x.shape[-1])
    p = p.reshape(depth, t, p.shape[-1])
    row = lambda a: a.reshape(1, -1)
    states = []
    k_new = v_new = k_b = v_b = None
    for i in range(depth):
        ple = (row(w["g_ple"][i]), w["w_ple_gate"][i], w["w_ple_up"][i])
        g_final = row(w["g_final"]) if i == depth - 1 else None
        if i < n_a:
            q, k, v, gate, gk = _gla_in(h, row(w["g_pre"][i]), w["gla_w_main"][i], w["gla_w_lr"][i],
                                        w["gla_w_gk2"][i], row(w["gla_b_gk"][i]), dk=dk, dv=dv, hk=hk)
            if gla_state is None:
                o, s = _gla_rec(q, k, v, gk, batch=batch, seq=seq, heads=heads, hk=hk, hv=hv)
            else:
                o, s = _gla_step(q, k, v, gk, gla_state, i, heads=heads, hk=hk, hv=hv)
            states.append(s)
            kv = (row(w["g_kv"]), w["w_kv_t"]) if i == n_a - 1 else None
            res = _mix_out(o, gate, h, p, i, w["gla_w_out"][i], *ple, g_out=row(jnp.tile(w["gla_g_out"][i], heads)),
                           kv=kv, g_final=g_final, hv=hv, seq=seq)
            h = res[0]
            if kv is not None:
                k_new, v_new = res[1:3]
                if seq > 1:
                    k_b, v_b = res[3:5]
        else:
            j = i - n_a
            q, gate = _sb_in(h, row(w["g_pre"][i]), w["sb_w_in"][j], hd=hd)
            if past is None:
                o = _sb_prompt(q, k_b, v_b, w["sb_bias"][j], batch=batch, seq=seq, hd=hd)
            else:
                o = _sb_decode(q.astype(F32), *past, w["sb_bias"][j], heads=sb_heads, hd=hd)
            h = _mix_out(o, gate, h, p, i, w["sb_w_out"][j], *ple, g_final=g_final)[0]
    return h, jnp.stack(states), k_new, v_new


def kernel(x_prompt, x_sample, state_gla, cache_k, cache_v, page_table, p_prompt, p_sample, g_pre, w_ple_gate,
           w_ple_up, g_ple, gla_w_in, gla_w_gk2, gla_b_gk, gla_g_out, gla_w_out, g_kv, w_kv, sb_w_in, sb_bias,
           sb_w_out, g_final):
    batch, seq, d = x_prompt.shape
    n_dec, dec_seq, _ = x_sample.shape
    assert dec_seq == 1
    _, _, heads, hk, hv = state_gla.shape
    sb_heads, hd = cache_k.shape[2:]
    dk, dv = heads * hk, heads * hv
    rank = gla_w_gk2.shape[1]
    main = 2 * dk + 2 * dv
    bf = lambda a: a.astype(BF16)
    w = dict(
        g_pre=g_pre, g_ple=g_ple, g_kv=g_kv, g_final=g_final, gla_b_gk=gla_b_gk, gla_g_out=gla_g_out,
        sb_bias=sb_bias, gla_w_in=gla_w_in,
        w_ple_gate=bf(w_ple_gate), w_ple_up=bf(w_ple_up), gla_w_out=bf(gla_w_out), w_kv_t=bf(w_kv.T),
        sb_w_in=bf(sb_w_in), sb_w_out=bf(sb_w_out),
        gla_w_main=bf(gla_w_in[:, :, :main]),
        gla_w_lr=bf(jnp.pad(gla_w_in[:, :, main:], ((0, 0), (0, 0), (0, LANES - rank)))),
        gla_w_gk2=bf(jnp.pad(gla_w_gk2, ((0, 0), (0, LANES - rank), (0, 0)))),
        gla_dims=(heads, hk, hv), sb_dims=(sb_heads, hd),
    )
    y_p, s_p, k_p, v_p = _run_group(x_prompt, p_prompt, w, batch=batch, seq=seq, gla_state=None, past=None)
    y_s, s_s, k_s, v_s = _run_group(x_sample, p_sample, w, batch=n_dec, seq=1, gla_state=state_gla,
                                    past=(cache_k, cache_v, page_table))
    from_t = lambda a: jnp.transpose(a.reshape(batch, sb_heads, hd, seq), (0, 3, 1, 2))
    from_rows = lambda a: a.reshape(n_dec, 1, sb_heads, hd)
    return (y_p.reshape(batch, seq, d), y_s.reshape(n_dec, 1, d), s_p, s_s,
            from_t(k_p), from_t(v_p), from_rows(k_s), from_rows(v_s))
```

```python
import functools

import numpy as np
import jax
import jax.numpy as jnp
from jax import lax
from jax.experimental import pallas as pl
from jax.experimental.pallas import tpu as pltpu

F32 = jnp.float32
BF16 = jnp.bfloat16

EPS = 1e-6
GATE_NORM = 16.0
GLA_CHUNK = 64
LANES = 128
VMEM_LIMIT = 48 * 1024 * 1024

TOKEN_BLOCK = 512
GLA_BLOCK = 256
SB_BLOCK = 256
SB_GROUP = 3
LOG2E = 1.4426950408889634
MASKED_LOG = -1e30
DECODE_PAGES = 16


def _dot(a, b):
    return jnp.dot(a, b, preferred_element_type=F32)


def _dot_nt(a, b):
    return lax.dot_general(a, b, (((1,), (1,)), ((), ())), preferred_element_type=F32)


def _dot_tn(a, b):
    return lax.dot_general(a, b, (((0,), (0,)), ((), ())), preferred_element_type=F32)


def _split_bf16(x):
    hi = x.astype(BF16)
    lo = (x - hi.astype(F32)).astype(BF16)
    return hi, lo


def _rms(x, g):
    ms = jnp.mean(x * x, axis=-1, keepdims=True)
    return x * lax.rsqrt(ms + EPS) * g


def _softplus_neg_abs(z):
    return jnp.log(1.0 + jnp.exp(-jnp.abs(z)))


def _log_sigmoid(z):
    return jnp.minimum(z, 0.0) - _softplus_neg_abs(z)


def _sigmoid(z):
    return 1.0 / (1.0 + jnp.exp(-z))


def _params(*sem):
    return pltpu.CompilerParams(dimension_semantics=sem, vmem_limit_bytes=VMEM_LIMIT)


def _const_spec(shape):
    nd = len(shape)
    return pl.BlockSpec(shape, lambda *_: (0,) * nd)


def _gla_in_kernel(x_ref, g_ref, w_ref, wlr_ref, wgk2_ref, bgk_ref,
                   q_ref, k_ref, v_ref, gate_ref, gk_ref, *, dk, dv, qscale):
    hn = _rms(x_ref[...], g_ref[...]).astype(BF16)
    q_ref[...] = _dot(hn, w_ref[:, 0:dk]) * qscale
    k_ref[...] = _dot(hn, w_ref[:, dk:2 * dk])
    v_ref[...] = _dot(hn, w_ref[:, 2 * dk:2 * dk + dv]).astype(BF16)
    gate_ref[...] = _dot(hn, w_ref[:, 2 * dk + dv:2 * dk + 2 * dv])
    lr = _dot(hn, wlr_ref[...]).astype(BF16)
    gk_ref[...] = _log_sigmoid(_dot(lr, wgk2_ref[...]) + bgk_ref[...]) / GATE_NORM


def _gla_in(x, g_pre, w_main, w_lr, w_gk2, b_gk, *, dk, dv, hk):
    t, d = x.shape
    tm = min(TOKEN_BLOCK, t)
    row = lambda i: (i, 0)
    return pl.pallas_call(
        functools.partial(_gla_in_kernel, dk=dk, dv=dv, qscale=float(hk) ** -0.5),
        grid=(t // tm,),
        in_specs=[pl.BlockSpec((tm, d), row), _const_spec(g_pre.shape), _const_spec(w_main.shape),
                  _const_spec(w_lr.shape), _const_spec(w_gk2.shape), _const_spec(b_gk.shape)],
        out_specs=[pl.BlockSpec((tm, dk), row), pl.BlockSpec((tm, dk), row), pl.BlockSpec((tm, dv), row),
                   pl.BlockSpec((tm, dv), row), pl.BlockSpec((tm, dk), row)],
        out_shape=[jax.ShapeDtypeStruct((t, dk), F32), jax.ShapeDtypeStruct((t, dk), F32),
                   jax.ShapeDtypeStruct((t, dv), BF16), jax.ShapeDtypeStruct((t, dv), F32),
                   jax.ShapeDtypeStruct((t, dk), F32)],
        compiler_params=_params("parallel"),
        name="gla_in",
    )(x, g_pre, w_main, w_lr, w_gk2, b_gk)


def _gla_rec_kernel(q_ref, k_ref, v_ref, gk_ref, lmat_ref, o_ref, s_ref, st_ref, *, tb, chunk):
    step = pl.program_id(1)
    heads, hv, hk = st_ref.shape
    heads_r = range(heads)

    @pl.when(step == 0)
    def _():
        st_ref[...] = jnp.zeros_like(st_ref)

    hi, lo = _split_bf16(gk_ref[...])
    lmat = lmat_ref[...]
    bb = _dot(lmat, hi) + _dot(lmat, lo)
    b, bl = bb[:tb], bb[tb:]
    q_in = (q_ref[...] * jnp.exp(b)).astype(BF16)
    k = k_ref[...]
    k_in = (k * jnp.exp(-b)).astype(BF16)
    k_out = (k * jnp.exp(bl - b)).astype(BF16)
    ks = [slice(h * hk, (h + 1) * hk) for h in heads_r]
    vs = [slice(h * hv, (h + 1) * hv) for h in heads_r]

    r = lax.broadcasted_iota(jnp.int32, (tb, tb), 0)
    c = lax.broadcasted_iota(jnp.int32, (tb, tb), 1)
    causal = (c <= r) & ((r // chunk) == (c // chunk))
    scores = [_dot_nt(q_in[:, ks[h]], k_in[:, ks[h]]) for h in heads_r]
    atts = [jnp.where(causal, s, 0.0).astype(BF16) for s in scores]
    o_intra = [_dot(atts[h], v_ref[:, vs[h]]) for h in heads_r]

    sts = [st_ref[h] for h in heads_r]
    for ci in range(tb // chunk):
        rows = slice(ci * chunk, (ci + 1) * chunk)
        o_inter = [_dot_nt(q_in[rows, ks[h]], sts[h].astype(BF16)) for h in heads_r]
        updates = [_dot_tn(v_ref[rows, vs[h]], k_out[rows, ks[h]]) for h in heads_r]
        decay = jnp.exp(bl[ci * chunk:ci * chunk + 1])
        for h in heads_r:
            o_ref[rows, vs[h]] = o_intra[h][rows] + o_inter[h]
            sts[h] = sts[h] * decay[:, ks[h]] + updates[h]
    for h in heads_r:
        st_ref[h] = sts[h]

    @pl.when(step == pl.num_programs(1) - 1)
    def _():
        for h in heads_r:
            s_ref[0, h] = sts[h].T


def _gla_lmat(tb, chunk):
    r = np.arange(tb)[:, None]
    c = np.arange(tb)[None, :]
    same = (r // chunk) == (c // chunk)
    return jnp.asarray(np.concatenate([same & (c <= r), same], axis=0), dtype=BF16)


def _gla_rec(q, k, v, gk, *, batch, seq, heads, hk, hv):
    tb = min(GLA_BLOCK, seq)
    nl = seq // tb
    tok = lambda b, l: (b * nl + l, 0)
    dk, dv = heads * hk, heads * hv
    return pl.pallas_call(
        functools.partial(_gla_rec_kernel, tb=tb, chunk=min(GLA_CHUNK, seq)),
        grid=(batch, nl),
        in_specs=[pl.BlockSpec((tb, dk), tok), pl.BlockSpec((tb, dk), tok), pl.BlockSpec((tb, dv), tok),
                  pl.BlockSpec((tb, dk), tok), _const_spec((2 * tb, tb))],
        out_specs=[pl.BlockSpec((tb, dv), tok), pl.BlockSpec((1, heads, hk, hv), lambda b, l: (b, 0, 0, 0))],
        out_shape=[jax.ShapeDtypeStruct((batch * seq, dv), F32),
                   jax.ShapeDtypeStruct((batch, heads, hk, hv), F32)],
        scratch_shapes=[pltpu.VMEM((heads, hv, hk), F32)],
        compiler_params=_params("parallel", "arbitrary"),
        name="gla_rec",
    )(q, k, v, gk, _gla_lmat(tb, min(GLA_CHUNK, seq)))


def _gla_step_kernel(q_ref, k_ref, v_ref, gk_ref, s_ref, o_ref, so_ref, *, heads, hk, hv):
    eye = lax.broadcasted_iota(jnp.int32, (hk, hk), 0) == lax.broadcasted_iota(jnp.int32, (hk, hk), 1)

    def column(row):
        return jnp.sum(jnp.where(eye, jnp.broadcast_to(row, (hk, hk)), 0.0), axis=1, keepdims=True)

    for h in range(heads):
        ks = slice(h * hk, (h + 1) * hk)
        vs = slice(h * hv, (h + 1) * hv)
        s_new = s_ref[0, h] * column(jnp.exp(gk_ref[0, :, ks])) + column(k_ref[0, :, ks]) * v_ref[0, :, vs]
        so_ref[0, h] = s_new
        o_ref[0, :, vs] = jnp.sum(column(q_ref[0, :, ks]) * s_new, axis=0, keepdims=True)


def _gla_step(q, k, v, gk, states, layer, *, heads, hk, hv):
    n = q.shape[0]
    vec = lambda a: a.reshape(n, 1, a.shape[-1])
    spec = lambda w: pl.BlockSpec((1, 1, w), lambda b: (b, 0, 0))
    sspec = pl.BlockSpec((1, heads, hk, hv), lambda b: (b, 0, 0, 0))
    o, s = pl.pallas_call(
        functools.partial(_gla_step_kernel, heads=heads, hk=hk, hv=hv),
        grid=(n,),
        in_specs=[spec(heads * hk), spec(heads * hk), spec(heads * hv), spec(heads * hk),
                  pl.BlockSpec((None, 1, heads, hk, hv), lambda b: (layer, b, 0, 0, 0))],
        out_specs=[spec(heads * hv), sspec],
        out_shape=[jax.ShapeDtypeStruct((n, 1, heads * hv), F32), jax.ShapeDtypeStruct(states.shape[1:], F32)],
        compiler_params=_params("parallel"),
        name="gla_step",
    )(vec(q), vec(k), vec(v.astype(F32)), vec(gk), states)
    return o.reshape(n, heads * hv), s


def _mix_out_kernel(*refs, head_norm, emit_kv, final_norm, hv):
    refs = list(refs)
    o_ref, gate_ref, h_ref, p_ref, wout_ref, gple_ref, wgate_ref, wup_ref = refs[:8]
    pos = 8
    o = o_ref[...].astype(F32)
    if head_norm:
        gout = refs[pos][...]
        pos += 1
        segs = []
        for s in range(o.shape[1] // hv):
            seg = o[:, s * hv:(s + 1) * hv]
            segs.append(seg * lax.rsqrt(jnp.mean(seg * seg, axis=-1, keepdims=True) + EPS))
        o = jnp.concatenate(segs, axis=1) * gout
    gate = gate_ref[...]
    mixed = (o * (gate * _sigmoid(gate))).astype(BF16)
    h1 = h_ref[...] + _dot(mixed, wout_ref[...])
    ple_gate = _sigmoid(_dot(_rms(h1, gple_ref[...]).astype(BF16), wgate_ref[...]))
    h2 = h1 + ple_gate * _dot(p_ref[...].astype(BF16), wup_ref[...])
    if emit_kv:
        gkv_ref, wkv_ref = refs[pos:pos + 2]
        pos += 2
    if final_norm:
        gfin_ref = refs[pos]
        pos += 1
    y_ref = refs[pos]
    pos += 1
    y_ref[...] = _rms(h2, gfin_ref[...]) if final_norm else h2
    if emit_kv == "transposed":
        kf_ref, vf_ref, kb_ref, vb_ref = refs[pos:pos + 4]
        width = kf_ref.shape[1]
        hkv = _rms(h2, gkv_ref[...]).astype(BF16)
        kk = _dot_nt(wkv_ref[0:width, :], hkv)
        vv = _dot_nt(wkv_ref[width:2 * width, :], hkv)
        kf_ref[0] = kk
        vf_ref[0] = vv
        kb_ref[0] = kk.astype(BF16)
        vb_ref[0] = vv.astype(BF16)
    elif emit_kv == "rows":
        kf_ref, vf_ref = refs[pos:pos + 2]
        width = kf_ref.shape[1]
        hkv = _rms(h2, gkv_ref[...]).astype(BF16)
        kf_ref[...] = _dot_nt(hkv, wkv_ref[0:width, :])
        vf_ref[...] = _dot_nt(hkv, wkv_ref[width:2 * width, :])


def _mix_out(o, gate, h, p, layer, w_out, g_ple, w_gate, w_up, *, g_out=None, kv=None, g_final=None, hv=0, seq=1):
    t, d = h.shape
    tm = min(TOKEN_BLOCK, t)
    row = lambda i: (i, 0)
    emit_kv = None if kv is None else ("transposed" if seq > 1 else "rows")
    args = [o, gate, h, p, w_out, g_ple, w_gate, w_up]
    in_specs = [pl.BlockSpec((tm, o.shape[1]), row), pl.BlockSpec((tm, d), row), pl.BlockSpec((tm, d), row),
                pl.BlockSpec((None, tm, p.shape[2]), lambda i: (layer, i, 0)), _const_spec(w_out.shape),
                _const_spec(g_ple.shape),
                _const_spec(w_gate.shape), _const_spec(w_up.shape)]
    if g_out is not None:
        args.append(g_out)
        in_specs.append(_const_spec(g_out.shape))
    if kv is not None:
        args.extend(kv)
        in_specs.extend([_const_spec(kv[0].shape), _const_spec(kv[1].shape)])
    if g_final is not None:
        args.append(g_final)
        in_specs.append(_const_spec(g_final.shape))
    out_specs = [pl.BlockSpec((tm, d), row)]
    out_shape = [jax.ShapeDtypeStruct((t, d), F32)]
    if emit_kv == "transposed":
        width = kv[1].shape[0] // 2
        blocks = seq // tm
        out_specs += [pl.BlockSpec((1, width, tm), lambda i: (i // blocks, 0, i % blocks))] * 4
        out_shape += ([jax.ShapeDtypeStruct((t // seq, width, seq), F32)] * 2
                      + [jax.ShapeDtypeStruct((t // seq, width, seq), BF16)] * 2)
    elif emit_kv == "rows":
        width = kv[1].shape[0] // 2
        out_specs += [pl.BlockSpec((tm, width), row)] * 2
        out_shape += [jax.ShapeDtypeStruct((t, width), F32)] * 2
    return pl.pallas_call(
        functools.partial(_mix_out_kernel, head_norm=g_out is not None, emit_kv=emit_kv,
                          final_norm=g_final is not None, hv=hv),
        grid=(t // tm,),
        in_specs=in_specs, out_specs=out_specs, out_shape=out_shape,
        compiler_params=_params("parallel"),
        name="mix_out",
    )(*args)


def _sb_in_kernel(x_ref, g_ref, w_ref, q_ref, gate_ref, *, width, scale):
    hn = _rms(x_ref[...], g_ref[...]).astype(BF16)
    q_ref[...] = (_dot(hn, w_ref[:, 0:width]) * scale).astype(BF16)
    gate_ref[...] = _dot(hn, w_ref[:, width:2 * width])


def _sb_in(x, g_pre, w_in, *, hd):
    t, d = x.shape
    width = w_in.shape[1] // 2
    tm = min(TOKEN_BLOCK, t)
    row = lambda i: (i, 0)
    return pl.pallas_call(
        functools.partial(_sb_in_kernel, width=width, scale=float(hd) ** -0.5 * LOG2E),
        grid=(t // tm,),
        in_specs=[pl.BlockSpec((tm, d), row), _const_spec(g_pre.shape), _const_spec(w_in.shape)],
        out_specs=[pl.BlockSpec((tm, width), row), pl.BlockSpec((tm, width), row)],
        out_shape=[jax.ShapeDtypeStruct((t, width), BF16), jax.ShapeDtypeStruct((t, width), F32)],
        compiler_params=_params("parallel"),
        name="sb_in",
    )(x, g_pre, w_in)


def _later_matrix(n):
    j = np.arange(n)[:, None]
    s = np.arange(n)[None, :]
    return jnp.asarray(j > s, dtype=BF16)


def _sb_tile_plan(nq, group):
    tiles = [(qi, qi - t, 1 if t == 0 else 0) for qi in range(nq) for t in range(qi + 1)]
    n_groups = 2 * -(-len(tiles) // (2 * group))
    tiles += [(0, 0, 2)] * (n_groups * group - len(tiles))
    plan = np.asarray(tiles, dtype=np.int32)
    return n_groups, jnp.asarray(plan[:, 0]), jnp.asarray(plan[:, 1]), jnp.asarray(plan[:, 2])


def _sb_prompt_kernel(bias_ref, tq_ref, tk_ref, tm_ref, q_ref, kt_ref, vt_ref, u_ref, o_ref,
                      qm_ref, bm_ref, rsum_ref, logit_ref, acc_ref, sum_ref,
                      *, blk, hd, group, n_groups):
    pair = pl.program_id(1)
    heads_per_block = LANES // hd
    heads = range(heads_per_block)
    nq = q_ref.shape[0] // blk
    lane = lax.broadcasted_iota(jnp.int32, (1, LANES), 1)

    strictly_past = (lax.broadcasted_iota(jnp.int32, (blk, blk), 1) < lax.broadcasted_iota(jnp.int32, (blk, blk), 0))
    for h in heads:
        q = q_ref[...]
        qm_ref[h] = jnp.where((lane >= h * hd) & (lane < (h + 1) * hd), q, jnp.zeros_like(q))
        bias = bias_ref[pair * heads_per_block + h] * LOG2E
        bm_ref[h, 0] = jnp.full((blk, blk), bias, F32)
        bm_ref[h, 1] = jnp.where(strictly_past, bias, MASKED_LOG)
        bm_ref[h, 2] = jnp.full((blk, blk), MASKED_LOG, F32)
    acc_ref[...] = jnp.zeros_like(acc_ref)
    sum_ref[...] = jnp.zeros_like(sum_ref)

    def block(ref, j):
        return ref[0, :, pl.ds(pl.multiple_of(j * blk, blk), blk)]

    def tiles(g):
        return [g * group + u for u in range(group)]

    def stage_a_scores(g):
        zs = []
        for n in tiles(g):
            kt = block(kt_ref, tk_ref[n])
            rows = pl.ds(pl.multiple_of(tq_ref[n] * blk, blk), blk)
            zs.append([_dot(qm_ref[h, rows, :], kt) for h in heads])
        return zs

    def stage_a(g, slot):
        zs = stage_a_scores(g)
        for u, n in enumerate(tiles(g)):
            for h in heads:
                z = zs[u][h] + bm_ref[h, tm_ref[n]]
                neg_part = jnp.minimum(z, 0.0)
                neg_relu = neg_part - z
                softplus = jnp.log2(1.0 + jnp.exp2(neg_part + neg_relu))
                log_keep = neg_relu - softplus
                logit_ref[slot, u, h] = _dot(log_keep.astype(BF16), u_ref[...]) + (neg_part - softplus)
                rsum_ref[slot, u, h] = jnp.broadcast_to(jnp.sum(log_keep, axis=1, keepdims=True), (blk, LANES))

    def stage_c_values(g, slot):
        outs = []
        for u, n in enumerate(tiles(g)):
            qi = tq_ref[n]
            vt = block(vt_ref, tk_ref[n])
            for h in heads:
                later_sum = sum_ref[qi, h]
                a = jnp.exp2(logit_ref[slot, u, h] + jnp.concatenate([later_sum] * (blk // LANES), axis=1))
                outs.append(_dot_nt(a.astype(BF16), vt))
                sum_ref[qi, h] = later_sum + rsum_ref[slot, u, h]
        return outs

    def stage_c_accumulate(g, outs):
        for u, n in enumerate(tiles(g)):
            for h in heads:
                acc_ref[tq_ref[n], h] += outs[u * heads_per_block + h]

    def iteration(g, parity):
        outs = stage_c_values(g - 1, 1 - parity)
        stage_a(g, parity)
        stage_c_accumulate(g - 1, outs)

    stage_a(0, 0)

    def body(i, _):
        iteration(2 * i + 1, 1)
        iteration(2 * i + 2, 0)
        return 0

    lax.fori_loop(0, n_groups // 2 - 1, body, 0)
    iteration(n_groups - 1, 1)
    stage_c_accumulate(n_groups - 1, stage_c_values(n_groups - 1, 1))

    for qi in range(nq):
        o = acc_ref[qi, heads_per_block - 1]
        for h in range(heads_per_block - 2, -1, -1):
            o = jnp.where(lane < (h + 1) * hd, acc_ref[qi, h], o)
        o_ref[qi * blk:(qi + 1) * blk, :] = o


def _sb_prompt(q, kt, vt, bias, *, batch, seq, hd):
    t, width = q.shape
    blk = min(SB_BLOCK, seq)
    nq = seq // blk
    group = SB_GROUP
    heads_per_block = LANES // hd
    n_groups, tile_q, tile_k, tile_mode = _sb_tile_plan(nq, group)
    smem = pl.BlockSpec(memory_space=pltpu.SMEM)
    qspec = pl.BlockSpec((seq, LANES), lambda b, p: (b, p))
    kvspec = pl.BlockSpec((1, LANES, seq), lambda b, p: (b, p, 0))
    tile = (group, heads_per_block, blk)
    return pl.pallas_call(
        functools.partial(_sb_prompt_kernel, blk=blk, hd=hd, group=group, n_groups=n_groups),
        grid=(batch, width // LANES),
        in_specs=[smem, smem, smem, smem, qspec, kvspec, kvspec, _const_spec((blk, blk))],
        out_specs=qspec,
        out_shape=jax.ShapeDtypeStruct((t, width), F32),
        scratch_shapes=[pltpu.VMEM((heads_per_block, seq, LANES), BF16),
                        pltpu.VMEM((heads_per_block, 3, blk, blk), F32),
                        pltpu.VMEM((2,) + tile + (LANES,), F32),
                        pltpu.VMEM((2,) + tile + (blk,), F32),
                        pltpu.VMEM((nq, heads_per_block, blk, LANES), F32),
                        pltpu.VMEM((nq, heads_per_block, blk, LANES), F32)],
        compiler_params=_params("parallel", "parallel"),
        name="sb_prompt",
    )(bias, tile_q, tile_k, tile_mode, q, kt, vt, _later_matrix(blk))


def _sb_decode_kernel(pt_ref, q_ref, bias_ref, u_ref, *refs, pages, heads, hd):
    del pt_ref
    k_refs = refs[:pages]
    v_refs = refs[pages:2 * pages]
    o_ref = refs[2 * pages]
    qb_ref, acc_ref, sum_ref, z_ref, a_ref = refs[2 * pages + 1:]
    step = pl.program_id(1)
    page = u_ref.shape[0]
    eye = lax.broadcasted_iota(jnp.int32, (hd, hd), 0) == lax.broadcasted_iota(jnp.int32, (hd, hd), 1)

    @pl.when(step == 0)
    def _():
        for h in range(heads):
            q_col = jnp.sum(jnp.where(eye, jnp.broadcast_to(q_ref[0, :, h * hd:(h + 1) * hd], (hd, hd)), 0.0),
                            axis=1, keepdims=True)
            qb_ref[h] = jnp.broadcast_to(q_col, (hd, page))
        acc_ref[...] = jnp.zeros_like(acc_ref)
        sum_ref[...] = jnp.zeros_like(sum_ref)

    for h in range(heads):
        qb = qb_ref[h]
        for j in range(pages):
            prod = k_refs[pages - 1 - j][0, h] * qb
            z_ref[pl.ds(j * heads + h, 1), :] = jnp.sum(prod, axis=0, keepdims=True)

    z = z_ref[...] + bias_ref[...] * LOG2E
    sp = jnp.log2(1.0 + jnp.exp2(-jnp.abs(z)))
    neg_part = jnp.minimum(z, 0.0)
    log_p = neg_part - sp
    log_keep = (neg_part - z) - sp
    hi, lo = _split_bf16(log_keep)
    u = u_ref[...]
    later_in = _dot(hi, u) + _dot(lo, u)
    row_sum = jnp.sum(log_keep, axis=1, keepdims=True)
    later_sum = sum_ref[...]
    for j in range(pages):
        rows = slice(j * heads, (j + 1) * heads)
        a_ref[rows, :] = jnp.exp2(log_p[rows] + later_in[rows] + later_sum)
        later_sum = later_sum + row_sum[rows]
    sum_ref[...] = later_sum

    for h in range(heads):
        acc = acc_ref[h]
        for j in range(pages):
            a_row = a_ref[pl.ds(j * heads + h, 1), :]
            acc = acc + jnp.broadcast_to(a_row, (hd, page)) * v_refs[pages - 1 - j][0, h]
        acc_ref[h] = acc

    @pl.when(step == pl.num_programs(1) - 1)
    def _():
        for h in range(heads):
            o_col = jnp.sum(acc_ref[h], axis=-1, keepdims=True)
            o_ref[0, :, h * hd:(h + 1) * hd] = jnp.sum(jnp.where(eye, jnp.broadcast_to(o_col, (hd, hd)), 0.0),
                                                      axis=0, keepdims=True)


def _sb_decode(q, cache_k, cache_v, page_table, bias, *, heads, hd):
    n, width = q.shape
    page = cache_k.shape[1]
    n_pages = page_table.shape[1]
    pages = min(DECODE_PAGES, n_pages)
    steps = n_pages // pages
    ck = jnp.transpose(cache_k, (0, 2, 3, 1))
    cv = jnp.transpose(cache_v, (0, 2, 3, 1))

    def page_spec(i):
        return pl.BlockSpec((1, heads, hd, page),
                            lambda b, s, pt: (pt[b, (steps - 1 - s) * pages + i], 0, 0, 0))

    row_spec = pl.BlockSpec((1, 1, width), lambda b, s, pt: (b, 0, 0))
    grid_spec = pltpu.PrefetchScalarGridSpec(
        num_scalar_prefetch=1,
        grid=(n, steps),
        in_specs=[row_spec, pl.BlockSpec((pages * heads, 1), lambda b, s, pt: (0, 0)),
                  pl.BlockSpec((page, page), lambda b, s, pt: (0, 0))]
        + [page_spec(i) for i in range(pages)] * 2,
        out_specs=row_spec,
        scratch_shapes=[pltpu.VMEM((heads, hd, page), F32), pltpu.VMEM((heads, hd, page), F32),
                        pltpu.VMEM((heads, 1), F32), pltpu.VMEM((pages * heads, page), F32),
                        pltpu.VMEM((pages * heads, page), F32)],
    )
    o = pl.pallas_call(
        functools.partial(_sb_decode_kernel, pages=pages, heads=heads, hd=hd),
        grid_spec=grid_spec,
        out_shape=jax.ShapeDtypeStruct((n, 1, width), F32),
        compiler_params=_params("parallel", "arbitrary"),
        name="sb_decode",
    )(page_table, q.reshape(n, 1, width), jnp.tile(bias, pages).reshape(pages * heads, 1),
      _later_matrix(page), *([ck] * pages), *([cv] * pages))
    return o.reshape(n, width)


def _run_group(x, p, w, *, batch, seq, gla_state, past):
    depth = p.shape[0]
    n_a = w["gla_w_in"].shape[0]
    heads, hk, hv = w["gla_dims"]
    sb_heads, hd = w["sb_dims"]
    dk, dv = heads * hk, heads * hv
    t = batch * seq
    h = x.reshape(t, x.shape[-1])
    p = p.reshape(depth, t, p.shape[-1])
    row = lambda a: a.reshape(1, -1)
    states = []
    k_new = v_new = k_b = v_b = None
    for i in range(depth):
        ple = (row(w["g_ple"][i]), w["w_ple_gate"][i], w["w_ple_up"][i])
        g_final = row(w["g_final"]) if i == depth - 1 else None
        if i < n_a:
            q, k, v, gate, gk = _gla_in(h, row(w["g_pre"][i]), w["gla_w_main"][i], w["gla_w_lr"][i],
                                        w["gla_w_gk2"][i], row(w["gla_b_gk"][i]), dk=dk, dv=dv, hk=hk)
            if gla_state is None:
                o, s = _gla_rec(q, k, v, gk, batch=batch, seq=seq, heads=heads, hk=hk, hv=hv)
            else:
                o, s = _gla_step(q, k, v, gk, gla_state, i, heads=heads, hk=hk, hv=hv)
            states.append(s)
            kv = (row(w["g_kv"]), w["w_kv_t"]) if i == n_a - 1 else None
            res = _mix_out(o, gate, h, p, i, w["gla_w_out"][i], *ple, g_out=row(jnp.tile(w["gla_g_out"][i], heads)),
                           kv=kv, g_final=g_final, hv=hv, seq=seq)
            h = res[0]
            if kv is not None:
                k_new, v_new = res[1:3]
                if seq > 1:
                    k_b, v_b = res[3:5]
        else:
            j = i - n_a
            q, gate = _sb_in(h, row(w["g_pre"][i]), w["sb_w_in"][j], hd=hd)
            if past is None:
                o = _sb_prompt(q, k_b, v_b, w["sb_bias"][j], batch=batch, seq=seq, hd=hd)
            else:
                o = _sb_decode(q.astype(F32), *past, w["sb_bias"][j], heads=sb_heads, hd=hd)
            h = _mix_out(o, gate, h, p, i, w["sb_w_out"][j], *ple, g_final=g_final)[0]
    return h, jnp.stack(states), k_new, v_new


def kernel(x_prompt, x_sample, state_gla, cache_k, cache_v, page_table, p_prompt, p_sample, g_pre, w_ple_gate,
           w_ple_up, g_ple, gla_w_in, gla_w_gk2, gla_b_gk, gla_g_out, gla_w_out, g_kv, w_kv, sb_w_in, sb_bias,
           sb_w_out, g_final):
    batch, seq, d = x_prompt.shape
    n_dec, dec_seq, _ = x_sample.shape
    assert dec_seq == 1
    _, _, heads, hk, hv = state_gla.shape
    sb_heads, hd = cache_k.shape[2:]
    dk, dv = heads * hk, heads * hv
    rank = gla_w_gk2.shape[1]
    main = 2 * dk + 2 * dv
    bf = lambda a: a.astype(BF16)
    w = dict(
        g_pre=g_pre, g_ple=g_ple, g_kv=g_kv, g_final=g_final, gla_b_gk=gla_b_gk, gla_g_out=gla_g_out,
        sb_bias=sb_bias, gla_w_in=gla_w_in,
        w_ple_gate=bf(w_ple_gate), w_ple_up=bf(w_ple_up), gla_w_out=bf(gla_w_out), w_kv_t=bf(w_kv.T),
        sb_w_in=bf(sb_w_in), sb_w_out=bf(sb_w_out),
        gla_w_main=bf(gla_w_in[:, :, :main]),
        gla_w_lr=bf(jnp.pad(gla_w_in[:, :, main:], ((0, 0), (0, 0), (0, LANES - rank)))),
        gla_w_gk2=bf(jnp.pad(gla_w_gk2, ((0, 0), (0, LANES - rank), (0, 0)))),
        gla_dims=(heads, hk, hv), sb_dims=(sb_heads, hd),
    )
    y_p, s_p, k_p, v_p = _run_group(x_prompt, p_prompt, w, batch=batch, seq=seq, gla_state=None, past=None)
    y_s, s_s, k_s, v_s = _run_group(x_sample, p_sample, w, batch=n_dec, seq=1, gla_state=state_gla,
                                    past=(cache_k, cache_v, page_table))
    from_t = lambda a: jnp.transpose(a.reshape(batch, sb_heads, hd, seq), (0, 3, 1, 2))
    from_rows = lambda a: a.reshape(n_dec, 1, sb_heads, hd)
    return (y_p.reshape(batch, seq, d), y_s.reshape(n_dec, 1, d), s_p, s_s,
            from_t(k_p), from_t(v_p), from_rows(k_s), from_rows(v_s))
```

```python
import functools

import numpy as np
import jax
import jax.numpy as jnp
from jax import lax
from jax.experimental import pallas as pl
from jax.experimental.pallas import tpu as pltpu

F32 = jnp.float32
BF16 = jnp.bfloat16

EPS = 1e-6
GATE_NORM = 16.0
GLA_CHUNK = 64
LANES = 128
VMEM_LIMIT = 48 * 1024 * 1024

TOKEN_BLOCK = 512
GLA_BLOCK = 256
GLA_SEQS = 2
SB_BLOCK = 256
SB_GROUP = 3
LOG2E = 1.4426950408889634
MASKED_LOG = -1e30
DECODE_PAGES = 16


def _dot(a, b):
    return jnp.dot(a, b, preferred_element_type=F32)


def _dot_nt(a, b):
    return lax.dot_general(a, b, (((1,), (1,)), ((), ())), preferred_element_type=F32)


def _dot_tn(a, b):
    return lax.dot_general(a, b, (((0,), (0,)), ((), ())), preferred_element_type=F32)


def _split_bf16(x):
    hi = x.astype(BF16)
    lo = (x - hi.astype(F32)).astype(BF16)
    return hi, lo


def _rms(x, g):
    ms = jnp.mean(x * x, axis=-1, keepdims=True)
    return x * lax.rsqrt(ms + EPS) * g


def _softplus_neg_abs(z):
    return jnp.log(1.0 + jnp.exp(-jnp.abs(z)))


def _log_sigmoid(z):
    return jnp.minimum(z, 0.0) - _softplus_neg_abs(z)


def _sigmoid(z):
    return 1.0 / (1.0 + jnp.exp(-z))


def _params(*sem):
    return pltpu.CompilerParams(dimension_semantics=sem, vmem_limit_bytes=VMEM_LIMIT)


def _const_spec(shape):
    nd = len(shape)
    return pl.BlockSpec(shape, lambda *_: (0,) * nd)


def _gla_in_kernel(x_ref, g_ref, w_ref, wlr_ref, wgk2_ref, bgk_ref,
                   q_ref, k_ref, v_ref, gate_ref, gk_ref, *, dk, dv, qscale):
    hn = _rms(x_ref[...], g_ref[...]).astype(BF16)
    q_ref[...] = _dot(hn, w_ref[:, 0:dk]) * qscale
    k_ref[...] = _dot(hn, w_ref[:, dk:2 * dk])
    v_ref[...] = _dot(hn, w_ref[:, 2 * dk:2 * dk + dv]).astype(BF16)
    gate_ref[...] = _dot(hn, w_ref[:, 2 * dk + dv:2 * dk + 2 * dv])
    lr = _dot(hn, wlr_ref[...]).astype(BF16)
    gk_ref[...] = _log_sigmoid(_dot(lr, wgk2_ref[...]) + bgk_ref[...]) / GATE_NORM


def _gla_in(x, g_pre, w_main, w_lr, w_gk2, b_gk, *, dk, dv, hk):
    t, d = x.shape
    tm = min(TOKEN_BLOCK, t)
    row = lambda i: (i, 0)
    return pl.pallas_call(
        functools.partial(_gla_in_kernel, dk=dk, dv=dv, qscale=float(hk) ** -0.5),
        grid=(t // tm,),
        in_specs=[pl.BlockSpec((tm, d), row), _const_spec(g_pre.shape), _const_spec(w_main.shape),
                  _const_spec(w_lr.shape), _const_spec(w_gk2.shape), _const_spec(b_gk.shape)],
        out_specs=[pl.BlockSpec((tm, dk), row), pl.BlockSpec((tm, dk), row), pl.BlockSpec((tm, dv), row),
                   pl.BlockSpec((tm, dv), row), pl.BlockSpec((tm, dk), row)],
        out_shape=[jax.ShapeDtypeStruct((t, dk), F32), jax.ShapeDtypeStruct((t, dk), F32),
                   jax.ShapeDtypeStruct((t, dv), BF16), jax.ShapeDtypeStruct((t, dv), F32),
                   jax.ShapeDtypeStruct((t, dk), F32)],
        compiler_params=_params("parallel"),
        name="gla_in",
    )(x, g_pre, w_main, w_lr, w_gk2, b_gk)


def _gla_rec_kernel(q_ref, k_ref, v_ref, gk_ref, lmat_ref, o_ref, s_ref, st_ref, *, tb, chunk):
    step = pl.program_id(1)
    nb, heads, hv, hk = st_ref.shape
    chains = [(bi, h) for bi in range(nb) for h in range(heads)]
    side_by_side = lambda ref: jnp.concatenate([ref[bi] for bi in range(nb)], axis=1)

    @pl.when(step == 0)
    def _():
        st_ref[...] = jnp.zeros_like(st_ref)

    hi, lo = _split_bf16(side_by_side(gk_ref))
    lmat = lmat_ref[...]
    bb = _dot(lmat, hi) + _dot(lmat, lo)
    b, bl = bb[:tb], bb[tb:]
    q_in = (side_by_side(q_ref) * jnp.exp(b)).astype(BF16)
    k = side_by_side(k_ref)
    k_in = (k * jnp.exp(-b)).astype(BF16)
    k_out = (k * jnp.exp(bl - b)).astype(BF16)
    ks = {(bi, h): slice((bi * heads + h) * hk, (bi * heads + h + 1) * hk) for bi, h in chains}
    vs = [slice(h * hv, (h + 1) * hv) for h in range(heads)]

    r = lax.broadcasted_iota(jnp.int32, (tb, tb), 0)
    c = lax.broadcasted_iota(jnp.int32, (tb, tb), 1)
    causal = (c <= r) & ((r // chunk) == (c // chunk))
    scores = [_dot_nt(q_in[:, ks[ch]], k_in[:, ks[ch]]) for ch in chains]
    atts = [jnp.where(causal, s, 0.0).astype(BF16) for s in scores]
    o_intra = [_dot(atts[i], v_ref[bi, :, vs[h]]) for i, (bi, h) in enumerate(chains)]

    sts = [st_ref[bi, h] for bi, h in chains]
    for ci in range(tb // chunk):
        rows = slice(ci * chunk, (ci + 1) * chunk)
        o_inter = [_dot_nt(q_in[rows, ks[ch]], sts[i].astype(BF16)) for i, ch in enumerate(chains)]
        updates = [_dot_tn(v_ref[bi, rows, vs[h]], k_out[rows, ks[(bi, h)]]) for bi, h in chains]
        decay = jnp.exp(bl[ci * chunk:ci * chunk + 1])
        for i, (bi, h) in enumerate(chains):
            o_ref[bi, rows, vs[h]] = o_intra[i][rows] + o_inter[i]
            sts[i] = sts[i] * decay[:, ks[(bi, h)]] + updates[i]
    for i, (bi, h) in enumerate(chains):
        st_ref[bi, h] = sts[i]

    @pl.when(step == pl.num_programs(1) - 1)
    def _():
        for i, (bi, h) in enumerate(chains):
            s_ref[bi, h] = sts[i].T


def _gla_lmat(tb, chunk):
    r = np.arange(tb)[:, None]
    c = np.arange(tb)[None, :]
    same = (r // chunk) == (c // chunk)
    return jnp.asarray(np.concatenate([same & (c <= r), same], axis=0), dtype=BF16)


def _gla_rec(q, k, v, gk, *, batch, seq, heads, hk, hv):
    tb = min(GLA_BLOCK, seq)
    nb = GLA_SEQS if batch % GLA_SEQS == 0 else 1
    dk, dv = heads * hk, heads * hv
    per_seq = lambda a: a.reshape(batch, seq, a.shape[-1])
    tok = lambda width: pl.BlockSpec((nb, tb, width), lambda b, l: (b, l, 0))
    o, s = pl.pallas_call(
        functools.partial(_gla_rec_kernel, tb=tb, chunk=min(GLA_CHUNK, seq)),
        grid=(batch // nb, seq // tb),
        in_specs=[tok(dk), tok(dk), tok(dv), tok(dk), _const_spec((2 * tb, tb))],
        out_specs=[tok(dv), pl.BlockSpec((nb, heads, hk, hv), lambda b, l: (b, 0, 0, 0))],
        out_shape=[jax.ShapeDtypeStruct((batch, seq, dv), F32),
                   jax.ShapeDtypeStruct((batch, heads, hk, hv), F32)],
        scratch_shapes=[pltpu.VMEM((nb, heads, hv, hk), F32)],
        compiler_params=_params("parallel", "arbitrary"),
        name="gla_rec",
    )(per_seq(q), per_seq(k), per_seq(v), per_seq(gk), _gla_lmat(tb, min(GLA_CHUNK, seq)))
    return o.reshape(batch * seq, dv), s


def _gla_step_kernel(q_ref, k_ref, v_ref, gk_ref, s_ref, o_ref, so_ref, *, heads, hk, hv):
    eye = lax.broadcasted_iota(jnp.int32, (hk, hk), 0) == lax.broadcasted_iota(jnp.int32, (hk, hk), 1)

    def column(row):
        return jnp.sum(jnp.where(eye, jnp.broadcast_to(row, (hk, hk)), 0.0), axis=1, keepdims=True)

    for h in range(heads):
        ks = slice(h * hk, (h + 1) * hk)
        vs = slice(h * hv, (h + 1) * hv)
        s_new = s_ref[0, h] * column(jnp.exp(gk_ref[0, :, ks])) + column(k_ref[0, :, ks]) * v_ref[0, :, vs]
        so_ref[0, h] = s_new
        o_ref[0, :, vs] = jnp.sum(column(q_ref[0, :, ks]) * s_new, axis=0, keepdims=True)


def _gla_step(q, k, v, gk, states, layer, *, heads, hk, hv):
    n = q.shape[0]
    vec = lambda a: a.reshape(n, 1, a.shape[-1])
    spec = lambda w: pl.BlockSpec((1, 1, w), lambda b: (b, 0, 0))
    sspec = pl.BlockSpec((1, heads, hk, hv), lambda b: (b, 0, 0, 0))
    o, s = pl.pallas_call(
        functools.partial(_gla_step_kernel, heads=heads, hk=hk, hv=hv),
        grid=(n,),
        in_specs=[spec(heads * hk), spec(heads * hk), spec(heads * hv), spec(heads * hk),
                  pl.BlockSpec((None, 1, heads, hk, hv), lambda b: (layer, b, 0, 0, 0))],
        out_specs=[spec(heads * hv), sspec],
        out_shape=[jax.ShapeDtypeStruct((n, 1, heads * hv), F32), jax.ShapeDtypeStruct(states.shape[1:], F32)],
        compiler_params=_params("parallel"),
        name="gla_step",
    )(vec(q), vec(k), vec(v.astype(F32)), vec(gk), states)
    return o.reshape(n, heads * hv), s


def _mix_out_kernel(*refs, head_norm, emit_kv, final_norm, hv):
    refs = list(refs)
    o_ref, gate_ref, h_ref, p_ref, wout_ref, gple_ref, wgate_ref, wup_ref = refs[:8]
    pos = 8
    o = o_ref[...].astype(F32)
    if head_norm:
        gout = refs[pos][...]
        pos += 1
        segs = []
        for s in range(o.shape[1] // hv):
            seg = o[:, s * hv:(s + 1) * hv]
            segs.append(seg * lax.rsqrt(jnp.mean(seg * seg, axis=-1, keepdims=True) + EPS))
        o = jnp.concatenate(segs, axis=1) * gout
    gate = gate_ref[...]
    mixed = (o * (gate * _sigmoid(gate))).astype(BF16)
    h1 = h_ref[...] + _dot(mixed, wout_ref[...])
    ple_gate = _sigmoid(_dot(_rms(h1, gple_ref[...]).astype(BF16), wgate_ref[...]))
    h2 = h1 + ple_gate * _dot(p_ref[...].astype(BF16), wup_ref[...])
    if emit_kv:
        gkv_ref, wkv_ref = refs[pos:pos + 2]
        pos += 2
    if final_norm:
        gfin_ref = refs[pos]
        pos += 1
    y_ref = refs[pos]
    pos += 1
    y_ref[...] = _rms(h2, gfin_ref[...]) if final_norm else h2
    if emit_kv == "transposed":
        kf_ref, vf_ref, kb_ref, vb_ref = refs[pos:pos + 4]
        width = kf_ref.shape[1]
        hkv = _rms(h2, gkv_ref[...]).astype(BF16)
        kk = _dot_nt(wkv_ref[0:width, :], hkv)
        vv = _dot_nt(wkv_ref[width:2 * width, :], hkv)
        kf_ref[0] = kk
        vf_ref[0] = vv
        kb_ref[0] = kk.astype(BF16)
        vb_ref[0] = vv.astype(BF16)
    elif emit_kv == "rows":
        kf_ref, vf_ref = refs[pos:pos + 2]
        width = kf_ref.shape[1]
        hkv = _rms(h2, gkv_ref[...]).astype(BF16)
        kf_ref[...] = _dot_nt(hkv, wkv_ref[0:width, :])
        vf_ref[...] = _dot_nt(hkv, wkv_ref[width:2 * width, :])


def _mix_out(o, gate, h, p, layer, w_out, g_ple, w_gate, w_up, *, g_out=None, kv=None, g_final=None, hv=0, seq=1):
    t, d = h.shape
    tm = min(TOKEN_BLOCK, t)
    row = lambda i: (i, 0)
    emit_kv = None if kv is None else ("transposed" if seq > 1 else "rows")
    args = [o, gate, h, p, w_out, g_ple, w_gate, w_up]
    in_specs = [pl.BlockSpec((tm, o.shape[1]), row), pl.BlockSpec((tm, d), row), pl.BlockSpec((tm, d), row),
                pl.BlockSpec((None, tm, p.shape[2]), lambda i: (layer, i, 0)), _const_spec(w_out.shape),
                _const_spec(g_ple.shape),
                _const_spec(w_gate.shape), _const_spec(w_up.shape)]
    if g_out is not None:
        args.append(g_out)
        in_specs.append(_const_spec(g_out.shape))
    if kv is not None:
        args.extend(kv)
        in_specs.extend([_const_spec(kv[0].shape), _const_spec(kv[1].shape)])
    if g_final is not None:
        args.append(g_final)
        in_specs.append(_const_spec(g_final.shape))
    out_specs = [pl.BlockSpec((tm, d), row)]
    out_shape = [jax.ShapeDtypeStruct((t, d), F32)]
    if emit_kv == "transposed":
        width = kv[1].shape[0] // 2
        blocks = seq // tm
        out_specs += [pl.BlockSpec((1, width, tm), lambda i: (i // blocks, 0, i % blocks))] * 4
        out_shape += ([jax.ShapeDtypeStruct((t // seq, width, seq), F32)] * 2
                      + [jax.ShapeDtypeStruct((t // seq, width, seq), BF16)] * 2)
    elif emit_kv == "rows":
        width = kv[1].shape[0] // 2
        out_specs += [pl.BlockSpec((tm, width), row)] * 2
        out_shape += [jax.ShapeDtypeStruct((t, width), F32)] * 2
    return pl.pallas_call(
        functools.partial(_mix_out_kernel, head_norm=g_out is not None, emit_kv=emit_kv,
                          final_norm=g_final is not None, hv=hv),
        grid=(t // tm,),
        in_specs=in_specs, out_specs=out_specs, out_shape=out_shape,
        compiler_params=_params("parallel"),
        name="mix_out",
    )(*args)


def _sb_in_kernel(x_ref, g_ref, w_ref, q_ref, gate_ref, *, width, scale):
    hn = _rms(x_ref[...], g_ref[...]).astype(BF16)
    q_ref[...] = (_dot(hn, w_ref[:, 0:width]) * scale).astype(BF16)
    gate_ref[...] = _dot(hn, w_ref[:, width:2 * width])


def _sb_in(x, g_pre, w_in, *, hd):
    t, d = x.shape
    width = w_in.shape[1] // 2
    tm = min(TOKEN_BLOCK, t)
    row = lambda i: (i, 0)
    return pl.pallas_call(
        functools.partial(_sb_in_kernel, width=width, scale=float(hd) ** -0.5 * LOG2E),
        grid=(t // tm,),
        in_specs=[pl.BlockSpec((tm, d), row), _const_spec(g_pre.shape), _const_spec(w_in.shape)],
        out_specs=[pl.BlockSpec((tm, width), row), pl.BlockSpec((tm, width), row)],
        out_shape=[jax.ShapeDtypeStruct((t, width), BF16), jax.ShapeDtypeStruct((t, width), F32)],
        compiler_params=_params("parallel"),
        name="sb_in",
    )(x, g_pre, w_in)


def _later_matrix(n):
    j = np.arange(n)[:, None]
    s = np.arange(n)[None, :]
    return jnp.asarray(j > s, dtype=BF16)


def _sb_tile_plan(nq, group):
    tiles = [(qi, qi - t, 1 if t == 0 else 0) for qi in range(nq) for t in range(qi + 1)]
    n_groups = 2 * -(-len(tiles) // (2 * group))
    tiles += [(0, 0, 2)] * (n_groups * group - len(tiles))
    plan = np.asarray(tiles, dtype=np.int32)
    return n_groups, jnp.asarray(plan[:, 0]), jnp.asarray(plan[:, 1]), jnp.asarray(plan[:, 2])


def _sb_prompt_kernel(bias_ref, tq_ref, tk_ref, tm_ref, q_ref, kt_ref, vt_ref, u_ref, o_ref,
                      qm_ref, bm_ref, rsum_ref, logit_ref, acc_ref, sum_ref,
                      *, blk, hd, group, n_groups):
    pair = pl.program_id(1)
    heads_per_block = LANES // hd
    heads = range(heads_per_block)
    nq = q_ref.shape[0] // blk
    lane = lax.broadcasted_iota(jnp.int32, (1, LANES), 1)

    strictly_past = (lax.broadcasted_iota(jnp.int32, (blk, blk), 1) < lax.broadcasted_iota(jnp.int32, (blk, blk), 0))
    for h in heads:
        q = q_ref[...]
        qm_ref[h] = jnp.where((lane >= h * hd) & (lane < (h + 1) * hd), q, jnp.zeros_like(q))
        bias = bias_ref[pair * heads_per_block + h] * LOG2E
        bm_ref[h, 0] = jnp.full((blk, blk), bias, F32)
        bm_ref[h, 1] = jnp.where(strictly_past, bias, MASKED_LOG)
        bm_ref[h, 2] = jnp.full((blk, blk), MASKED_LOG, F32)
    acc_ref[...] = jnp.zeros_like(acc_ref)
    sum_ref[...] = jnp.zeros_like(sum_ref)

    def block(ref, j):
        return ref[0, :, pl.ds(pl.multiple_of(j * blk, blk), blk)]

    def tiles(g):
        return [g * group + u for u in range(group)]

    def stage_a_scores(g):
        zs = []
        for n in tiles(g):
            kt = block(kt_ref, tk_ref[n])
            rows = pl.ds(pl.multiple_of(tq_ref[n] * blk, blk), blk)
            zs.append([_dot(qm_ref[h, rows, :], kt) for h in heads])
        return zs

    def stage_a(g, slot):
        zs = stage_a_scores(g)
        for u, n in enumerate(tiles(g)):
            for h in heads:
                z = zs[u][h] + bm_ref[h, tm_ref[n]]
                neg_part = jnp.minimum(z, 0.0)
                neg_relu = neg_part - z
                softplus = jnp.log2(1.0 + jnp.exp2(neg_part + neg_relu))
                log_keep = neg_relu - softplus
                logit_ref[slot, u, h] = _dot(log_keep.astype(BF16), u_ref[...]) + (neg_part - softplus)
                rsum_ref[slot, u, h] = jnp.broadcast_to(jnp.sum(log_keep, axis=1, keepdims=True), (blk, LANES))

    def stage_c_values(g, slot):
        outs = []
        for u, n in enumerate(tiles(g)):
            qi = tq_ref[n]
            vt = block(vt_ref, tk_ref[n])
            for h in heads:
                later_sum = sum_ref[qi, h]
                a = jnp.exp2(logit_ref[slot, u, h] + jnp.concatenate([later_sum] * (blk // LANES), axis=1))
                outs.append(_dot_nt(a.astype(BF16), vt))
                sum_ref[qi, h] = later_sum + rsum_ref[slot, u, h]
        return outs

    def stage_c_accumulate(g, outs):
        for u, n in enumerate(tiles(g)):
            for h in heads:
                acc_ref[tq_ref[n], h] += outs[u * heads_per_block + h]

    def iteration(g, parity):
        outs = stage_c_values(g - 1, 1 - parity)
        stage_a(g, parity)
        stage_c_accumulate(g - 1, outs)

    stage_a(0, 0)

    def body(i, _):
        iteration(2 * i + 1, 1)
        iteration(2 * i + 2, 0)
        return 0

    lax.fori_loop(0, n_groups // 2 - 1, body, 0)
    iteration(n_groups - 1, 1)
    stage_c_accumulate(n_groups - 1, stage_c_values(n_groups - 1, 1))

    for qi in range(nq):
        o = acc_ref[qi, heads_per_block - 1]
        for h in range(heads_per_block - 2, -1, -1):
            o = jnp.where(lane < (h + 1) * hd, acc_ref[qi, h], o)
        o_ref[qi * blk:(qi + 1) * blk, :] = o


def _sb_prompt(q, kt, vt, bias, *, batch, seq, hd):
    t, width = q.shape
    blk = min(SB_BLOCK, seq)
    nq = seq // blk
    group = SB_GROUP
    heads_per_block = LANES // hd
    n_groups, tile_q, tile_k, tile_mode = _sb_tile_plan(nq, group)
    smem = pl.BlockSpec(memory_space=pltpu.SMEM)
    qspec = pl.BlockSpec((seq, LANES), lambda b, p: (b, p))
    kvspec = pl.BlockSpec((1, LANES, seq), lambda b, p: (b, p, 0))
    tile = (group, heads_per_block, blk)
    return pl.pallas_call(
        functools.partial(_sb_prompt_kernel, blk=blk, hd=hd, group=group, n_groups=n_groups),
        grid=(batch, width // LANES),
        in_specs=[smem, smem, smem, smem, qspec, kvspec, kvspec, _const_spec((blk, blk))],
        out_specs=qspec,
        out_shape=jax.ShapeDtypeStruct((t, width), F32),
        scratch_shapes=[pltpu.VMEM((heads_per_block, seq, LANES), BF16),
                        pltpu.VMEM((heads_per_block, 3, blk, blk), F32),
                        pltpu.VMEM((2,) + tile + (LANES,), F32),
                        pltpu.VMEM((2,) + tile + (blk,), F32),
                        pltpu.VMEM((nq, heads_per_block, blk, LANES), F32),
                        pltpu.VMEM((nq, heads_per_block, blk, LANES), F32)],
        compiler_params=_params("parallel", "parallel"),
        name="sb_prompt",
    )(bias, tile_q, tile_k, tile_mode, q, kt, vt, _later_matrix(blk))


def _sb_decode_kernel(pt_ref, q_ref, bias_ref, u_ref, *refs, pages, heads, hd):
    del pt_ref
    k_refs = refs[:pages]
    v_refs = refs[pages:2 * pages]
    o_ref = refs[2 * pages]
    qb_ref, acc_ref, sum_ref, z_ref, a_ref = refs[2 * pages + 1:]
    step = pl.program_id(1)
    page = u_ref.shape[0]
    eye = lax.broadcasted_iota(jnp.int32, (hd, hd), 0) == lax.broadcasted_iota(jnp.int32, (hd, hd), 1)

    @pl.when(step == 0)
    def _():
        for h in range(heads):
            q_col = jnp.sum(jnp.where(eye, jnp.broadcast_to(q_ref[0, :, h * hd:(h + 1) * hd], (hd, hd)), 0.0),
                            axis=1, keepdims=True)
            qb_ref[h] = jnp.broadcast_to(q_col, (hd, page))
        acc_ref[...] = jnp.zeros_like(acc_ref)
        sum_ref[...] = jnp.zeros_like(sum_ref)

    for h in range(heads):
        qb = qb_ref[h]
        for j in range(pages):
            prod = k_refs[pages - 1 - j][0, h] * qb
            z_ref[pl.ds(j * heads + h, 1), :] = jnp.sum(prod, axis=0, keepdims=True)

    z = z_ref[...] + bias_ref[...] * LOG2E
    sp = jnp.log2(1.0 + jnp.exp2(-jnp.abs(z)))
    neg_part = jnp.minimum(z, 0.0)
    log_p = neg_part - sp
    log_keep = (neg_part - z) - sp
    hi, lo = _split_bf16(log_keep)
    u = u_ref[...]
    later_in = _dot(hi, u) + _dot(lo, u)
    row_sum = jnp.sum(log_keep, axis=1, keepdims=True)
    later_sum = sum_ref[...]
    for j in range(pages):
        rows = slice(j * heads, (j + 1) * heads)
        a_ref[rows, :] = jnp.exp2(log_p[rows] + later_in[rows] + later_sum)
        later_sum = later_sum + row_sum[rows]
    sum_ref[...] = later_sum

    for h in range(heads):
        acc = acc_ref[h]
        for j in range(pages):
            a_row = a_ref[pl.ds(j * heads + h, 1), :]
            acc = acc + jnp.broadcast_to(a_row, (hd, page)) * v_refs[pages - 1 - j][0, h]
        acc_ref[h] = acc

    @pl.when(step == pl.num_programs(1) - 1)
    def _():
        for h in range(heads):
            o_col = jnp.sum(acc_ref[h], axis=-1, keepdims=True)
            o_ref[0, :, h * hd:(h + 1) * hd] = jnp.sum(jnp.where(eye, jnp.broadcast_to(o_col, (hd, hd)), 0.0),
                                                      axis=0, keepdims=True)


def _sb_decode(q, cache_k, cache_v, page_table, bias, *, heads, hd):
    n, width = q.shape
    page = cache_k.shape[1]
    n_pages = page_table.shape[1]
    pages = min(DECODE_PAGES, n_pages)
    steps = n_pages // pages
    ck = jnp.transpose(cache_k, (0, 2, 3, 1))
    cv = jnp.transpose(cache_v, (0, 2, 3, 1))

    def page_spec(i):
        return pl.BlockSpec((1, heads, hd, page),
                            lambda b, s, pt: (pt[b, (steps - 1 - s) * pages + i], 0, 0, 0))

    row_spec = pl.BlockSpec((1, 1, width), lambda b, s, pt: (b, 0, 0))
    grid_spec = pltpu.PrefetchScalarGridSpec(
        num_scalar_prefetch=1,
        grid=(n, steps),
        in_specs=[row_spec, pl.BlockSpec((pages * heads, 1), lambda b, s, pt: (0, 0)),
                  pl.BlockSpec((page, page), lambda b, s, pt: (0, 0))]
        + [page_spec(i) for i in range(pages)] * 2,
        out_specs=row_spec,
        scratch_shapes=[pltpu.VMEM((heads, hd, page), F32), pltpu.VMEM((heads, hd, page), F32),
                        pltpu.VMEM((heads, 1), F32), pltpu.VMEM((pages * heads, page), F32),
                        pltpu.VMEM((pages * heads, page), F32)],
    )
    o = pl.pallas_call(
        functools.partial(_sb_decode_kernel, pages=pages, heads=heads, hd=hd),
        grid_spec=grid_spec,
        out_shape=jax.ShapeDtypeStruct((n, 1, width), F32),
        compiler_params=_params("parallel", "arbitrary"),
        name="sb_decode",
    )(page_table, q.reshape(n, 1, width), jnp.tile(bias, pages).reshape(pages * heads, 1),
      _later_matrix(page), *([ck] * pages), *([cv] * pages))
    return o.reshape(n, width)


def _run_group(x, p, w, *, batch, seq, gla_state, past):
    depth = p.shape[0]
    n_a = w["gla_w_in"].shape[0]
    heads, hk, hv = w["gla_dims"]
    sb_heads, hd = w["sb_dims"]
    dk, dv = heads * hk, heads * hv
    t = batch * seq
    h = x.reshape(t, x.shape[-1])
    p = p.reshape(depth, t, p.shape[-1])
    row = lambda a: a.reshape(1, -1)
    states = []
    k_new = v_new = k_b = v_b = None
    for i in range(depth):
        ple = (row(w["g_ple"][i]), w["w_ple_gate"][i], w["w_ple_up"][i])
        g_final = row(w["g_final"]) if i == depth - 1 else None
        if i < n_a:
            q, k, v, gate, gk = _gla_in(h, row(w["g_pre"][i]), w["gla_w_main"][i], w["gla_w_lr"][i],
                                        w["gla_w_gk2"][i], row(w["gla_b_gk"][i]), dk=dk, dv=dv, hk=hk)
            if gla_state is None:
                o, s = _gla_rec(q, k, v, gk, batch=batch, seq=seq, heads=heads, hk=hk, hv=hv)
            else:
                o, s = _gla_step(q, k, v, gk, gla_state, i, heads=heads, hk=hk, hv=hv)
            states.append(s)
            kv = (row(w["g_kv"]), w["w_kv_t"]) if i == n_a - 1 else None
            res = _mix_out(o, gate, h, p, i, w["gla_w_out"][i], *ple, g_out=row(jnp.tile(w["gla_g_out"][i], heads)),
                           kv=kv, g_final=g_final, hv=hv, seq=seq)
            h = res[0]
            if kv is not None:
                k_new, v_new = res[1:3]
                if seq > 1:
                    k_b, v_b = res[3:5]
        else:
            j = i - n_a
            q, gate = _sb_in(h, row(w["g_pre"][i]), w["sb_w_in"][j], hd=hd)
            if past is None:
                o = _sb_prompt(q, k_b, v_b, w["sb_bias"][j], batch=batch, seq=seq, hd=hd)
            else:
                o = _sb_decode(q.astype(F32), *past, w["sb_bias"][j], heads=sb_heads, hd=hd)
            h = _mix_out(o, gate, h, p, i, w["sb_w_out"][j], *ple, g_final=g_final)[0]
    return h, jnp.stack(states), k_new, v_new


def kernel(x_prompt, x_sample, state_gla, cache_k, cache_v, page_table, p_prompt, p_sample, g_pre, w_ple_gate,
           w_ple_up, g_ple, gla_w_in, gla_w_gk2, gla_b_gk, gla_g_out, gla_w_out, g_kv, w_kv, sb_w_in, sb_bias,
           sb_w_out, g_final):
    batch, seq, d = x_prompt.shape
    n_dec, dec_seq, _ = x_sample.shape
    assert dec_seq == 1
    _, _, heads, hk, hv = state_gla.shape
    sb_heads, hd = cache_k.shape[2:]
    dk, dv = heads * hk, heads * hv
    rank = gla_w_gk2.shape[1]
    main = 2 * dk + 2 * dv
    bf = lambda a: a.astype(BF16)
    w = dict(
        g_pre=g_pre, g_ple=g_ple, g_kv=g_kv, g_final=g_final, gla_b_gk=gla_b_gk, gla_g_out=gla_g_out,
        sb_bias=sb_bias, gla_w_in=gla_w_in,
        w_ple_gate=bf(w_ple_gate), w_ple_up=bf(w_ple_up), gla_w_out=bf(gla_w_out), w_kv_t=bf(w_kv.T),
        sb_w_in=bf(sb_w_in), sb_w_out=bf(sb_w_out),
        gla_w_main=bf(gla_w_in[:, :, :main]),
        gla_w_lr=bf(jnp.pad(gla_w_in[:, :, main:], ((0, 0), (0, 0), (0, LANES - rank)))),
        gla_w_gk2=bf(jnp.pad(gla_w_gk2, ((0, 0), (0, LANES - rank), (0, 0)))),
        gla_dims=(heads, hk, hv), sb_dims=(sb_heads, hd),
    )
    y_p, s_p, k_p, v_p = _run_group(x_prompt, p_prompt, w, batch=batch, seq=seq, gla_state=None, past=None)
    y_s, s_s, k_s, v_s = _run_group(x_sample, p_sample, w, batch=n_dec, seq=1, gla_state=state_gla,
                                    past=(cache_k, cache_v, page_table))
    from_t = lambda a: jnp.transpose(a.reshape(batch, sb_heads, hd, seq), (0, 3, 1, 2))
    from_rows = lambda a: a.reshape(n_dec, 1, sb_heads, hd)
    return (y_p.reshape(batch, seq, d), y_s.reshape(n_dec, 1, d), s_p, s_s,
            from_t(k_p), from_t(v_p), from_rows(k_s), from_rows(v_s))
```
